```python
import math
import jax, jax.numpy as jnp
from jax import lax
import numpy as np

D_MODEL = 1024
BATCH = 16
SEQ = 4096
DEPTH = 1
DEC_BATCH = 8
DEC_SEQ = 4096
PAST_LEN = 128

RET_HEADS = 4
RET_DK = 128
RET_DV = 256
RET_CHUNK = 128
ATT_GROUPS = ((128, 1), (512, 4), (2048, 16))
ATT_HEADS_PER_GROUP = 8
ATT_HEADS = ATT_HEADS_PER_GROUP * len(ATT_GROUPS)
ATT_DH = 64
N_EXPERTS = 16
EC_CAPACITY_FACTOR = 2
EXPERT_HIDDEN = 2048
ROPE_THETA = 10000.0
EPS = 1e-6
N_BRANCHES = 2

RQ = RET_HEADS * RET_DK
RV = RET_HEADS * RET_DV
AQ = ATT_HEADS * ATT_DH
A_OUT = ATT_HEADS_PER_GROUP * ATT_DH
IN_COLS = 2 * RQ + 2 * RV + 3 * AQ + N_BRANCHES * D_MODEL

kernel_name = "hybrid_retention_dilated_attn_ec_moe_encoder"


def rmsnorm(x, gain):
    x32 = x.astype(jnp.float32)
    y = x32 * lax.rsqrt(jnp.mean(x32 * x32, axis=-1, keepdims=True) + EPS)
    return (y * gain.astype(jnp.float32)).astype(x.dtype)


def rotate(x, inv_freq):
    T = x.shape[1]
    pos = jnp.arange(T, dtype=jnp.float32)
    ang = pos[:, None] * inv_freq[None, :]
    cos = jnp.cos(ang)[None, :, None, :]
    sin = jnp.sin(ang)[None, :, None, :]
    x32 = x.astype(jnp.float32)
    x1, x2 = jnp.split(x32, 2, axis=-1)
    return jnp.concatenate([x1 * cos - x2 * sin, x2 * cos + x1 * sin], axis=-1).astype(x.dtype)


def retention_one_direction(q, k, v, log_gamma, include_diag):
    B, H, T, dk = q.shape
    dv = v.shape[-1]
    C = RET_CHUNK
    N = T // C
    lg = log_gamma.astype(jnp.float32)
    qc = q.reshape(B, H, N, C, dk)
    kc = k.reshape(B, H, N, C, dk)
    vc = v.reshape(B, H, N, C, dv)
    idx = jnp.arange(C, dtype=jnp.float32)
    diff = idx[:, None] - idx[None, :]
    mask = (diff >= 0) if include_diag else (diff > 0)
    decay = jnp.where(mask[None], jnp.exp(lg[:, None, None] * jnp.where(mask, diff, 0.0)[None]), 0.0)
    scores = jnp.einsum('bhncd,bhnsd->bhncs', qc, kc) * decay[None, :, None]
    out_inner = jnp.einsum('bhncs,bhnse->bhnce', scores, vc)
    k_dec = kc * jnp.exp(lg[:, None] * (C - 1 - idx)[None, :])[None, :, None, :, None]
    kv = jnp.einsum('bhncd,bhnce->nbhde', k_dec, vc)
    chunk_decay = jnp.exp(lg * C)[None, :, None, None]

    def step(state, kv_n):
        return state * chunk_decay + kv_n, state

    init = jnp.zeros((B, H, dk, dv), kv.dtype)
    _, states = lax.scan(step, init, kv)
    q_dec = qc * jnp.exp(lg[:, None] * (idx + 1.0)[None, :])[None, :, None, :, None]
    out_cross = jnp.einsum('bhncd,nbhde->bhnce', q_dec, states)
    return (out_inner + out_cross).reshape(B, H, T, dv)


def dilated_group_attention(q, k, v, window, dil):
    B, T, Hg, dh = q.shape
    L = T // dil
    n = window // (2 * dil)
    QB = n
    nb = -(-L // QB)
    Lp = nb * QB

    def to_sub(x):
        return x.reshape(B, L, dil, Hg, dh).transpose(0, 2, 3, 1, 4)

    qs, ks, vs = to_sub(q), to_sub(k), to_sub(v)
    qp = jnp.pad(qs, ((0, 0), (0, 0), (0, 0), (0, Lp - L), (0, 0)))
    kv_pad = ((0, 0), (0, 0), (0, 0), (n, Lp - L + n), (0, 0))
    kp = jnp.pad(ks, kv_pad).reshape(B, dil, Hg, nb + 2, QB, dh)
    vp = jnp.pad(vs, kv_pad).reshape(B, dil, Hg, nb + 2, QB, dh)
    kwin = jnp.concatenate([kp[:, :, :, :-2], kp[:, :, :, 1:-1], kp[:, :, :, 2:]], axis=4)
    vwin = jnp.concatenate([vp[:, :, :, :-2], vp[:, :, :, 1:-1], vp[:, :, :, 2:]], axis=4)
    qb = qp.reshape(B, dil, Hg, nb, QB, dh)
    s = jnp.einsum('brhnqe,brhnke->brhnqk', qb, kwin).astype(jnp.float32)
    qq = jnp.arange(QB)
    kk = jnp.arange(3 * QB)
    blk = jnp.arange(nb)
    band = jnp.abs(kk[None, :] - n - qq[:, None]) <= n
    key_pos = blk[:, None] * QB - n + kk[None, :]
    valid = (key_pos >= 0) & (key_pos < L)
    mask = band[None] & valid[:, None, :]
    s = jnp.where(mask[None, None, None], s, -jnp.inf)
    m = jnp.max(s, axis=-1)
    p = jnp.exp(s - m[..., None])
    l = jnp.sum(p, axis=-1)
    acc = jnp.einsum('brhnqk,brhnke->brhnqe', p, vwin.astype(jnp.float32))

    def from_sub(x):
        tail = x.shape[5:]
        x = x.reshape((B, dil, Hg, Lp) + tail)[:, :, :, :L]
        perm = (0, 3, 1, 2) + tuple(range(4, 4 + len(tail)))
        return x.transpose(perm).reshape((B, T, Hg) + tail)

    return from_sub(acc), from_sub(m), from_sub(l)


def token_mixer(h, w_in, ret_log_decay_fwd, ret_log_decay_bwd, ret_norm, w_ret_out, w_att_out, w_out):
    B, T, _ = h.shape
    proj = h @ w_in
    cuts = [RQ, 2 * RQ, 2 * RQ + RV, 2 * RQ + 2 * RV,
            2 * RQ + 2 * RV + AQ, 2 * RQ + 2 * RV + 2 * AQ, 2 * RQ + 2 * RV + 3 * AQ]
    rq, rk, rv, rg, aq, ak, av, gate_logits = jnp.split(proj, cuts, axis=-1)

    ret_inv = 1.0 / (ROPE_THETA ** jnp.linspace(0.0, 1.0, RET_DK // 2, dtype=jnp.float32))
    q = rotate(rq.reshape(B, T, RET_HEADS, RET_DK), ret_inv).transpose(0, 2, 1, 3)
    k = (rotate(rk.reshape(B, T, RET_HEADS, RET_DK), ret_inv) * (RET_DK ** -0.5)).transpose(0, 2, 1, 3)
    v = rv.reshape(B, T, RET_HEADS, RET_DV).transpose(0, 2, 1, 3)
    fwd = retention_one_direction(q, k, v, ret_log_decay_fwd, True)
    bwd = retention_one_direction(q[:, :, ::-1], k[:, :, ::-1], v[:, :, ::-1], ret_log_decay_bwd, False)[:, :, ::-1]
    ret = (fwd + bwd).transpose(0, 2, 1, 3)
    r32 = ret.astype(jnp.float32)
    r32 = r32 * lax.rsqrt(jnp.mean(r32 * r32, axis=-1, keepdims=True) + EPS)
    ret_n = (r32 * ret_norm.reshape(RET_HEADS, RET_DV).astype(jnp.float32)).reshape(B, T, RV).astype(h.dtype)
    branch_a = (jax.nn.silu(rg) * ret_n) @ w_ret_out

    att_inv = ROPE_THETA ** (-jnp.arange(0, ATT_DH, 2, dtype=jnp.float32) / ATT_DH)
    qa = rotate(aq.reshape(B, T, ATT_HEADS, ATT_DH), att_inv) * (ATT_DH ** -0.5)
    ka = rotate(ak.reshape(B, T, ATT_HEADS, ATT_DH), att_inv)
    va = av.reshape(B, T, ATT_HEADS, ATT_DH)
    accs, ms, ls = [], [], []
    for g, (window, dil) in enumerate(ATT_GROUPS):
        sl = slice(g * ATT_HEADS_PER_GROUP, (g + 1) * ATT_HEADS_PER_GROUP)
        acc, m, l = dilated_group_attention(qa[:, :, sl], ka[:, :, sl], va[:, :, sl], window, dil)
        accs.append(acc); ms.append(m); ls.append(l)
    m_all = jnp.stack(ms)
    m_max = jnp.max(m_all, axis=0)
    w = jnp.exp(m_all - m_max[None])
    num = jnp.sum(w[..., None] * jnp.stack(accs), axis=0)
    den = jnp.sum(w * jnp.stack(ls), axis=0)
    att = (num / den[..., None]).reshape(B, T, A_OUT).astype(h.dtype)
    branch_b = att @ w_att_out

    gates = jax.nn.sigmoid(gate_logits.reshape(B, T, N_BRANCHES, D_MODEL))
    merged = gates[:, :, 0] * branch_a + gates[:, :, 1] * branch_b
    return merged @ w_out


def expert_choice_ffn(h, w_router, w_exp_gate, w_exp_up, w_exp_down):
    B, T, D = h.shape
    N = B * T
    tok = h.reshape(N, D)
    aff = jax.nn.softmax((tok @ w_router).astype(jnp.float32), axis=-1)
    cap = EC_CAPACITY_FACTOR * N // N_EXPERTS
    gate, idx = lax.top_k(aff.T, cap)
    xs = tok[idx]
    hid = jax.nn.silu(jnp.einsum('ecd,edf->ecf', xs, w_exp_gate)) * jnp.einsum('ecd,edf->ecf', xs, w_exp_up)
    out = jnp.einsum('ecf,efd->ecd', hid, w_exp_down) * gate[..., None].astype(h.dtype)
    y = jnp.zeros((N, D), out.dtype).at[idx.reshape(-1)].add(out.reshape(-1, D))
    return y.reshape(B, T, D).astype(h.dtype)


def encoder(x, norm_mix, w_in, ret_log_decay_fwd, ret_log_decay_bwd, ret_norm, w_ret_out, w_att_out, w_out,
            norm_ffn, w_router, w_exp_gate, w_exp_up, w_exp_down, norm_final):
    for layer in range(DEPTH):
        h = rmsnorm(x, norm_mix[layer])
        x = x + token_mixer(h, w_in[layer], ret_log_decay_fwd[layer], ret_log_decay_bwd[layer], ret_norm[layer],
                            w_ret_out[layer], w_att_out[layer], w_out[layer])
        h = rmsnorm(x, norm_ffn[layer])
        x = x + expert_choice_ffn(h, w_router[layer], w_exp_gate[layer], w_exp_up[layer], w_exp_down[layer])
    return rmsnorm(x, norm_final)


def setup_inputs(seed: int = 0) -> dict:
    key = jax.random.key(seed)
    ks = jax.random.split(key, 20)
    f32 = jnp.float32

    def nrm(k, shape, scale):
        return jax.random.normal(k, shape, f32) * scale

    base_decay = jnp.log(1.0 - 2.0 ** (-5.0 - jnp.arange(RET_HEADS, dtype=f32)))
    return {
        "x_prompt": nrm(ks[0], (BATCH, SEQ, D_MODEL), 1.0),
        "x_sample": nrm(ks[1], (DEC_BATCH, DEC_SEQ, D_MODEL), 1.0),
        "norm_mix": 1.0 + nrm(ks[2], (DEPTH, D_MODEL), 0.02),
        "w_in": nrm(ks[3], (DEPTH, D_MODEL, IN_COLS), D_MODEL ** -0.5),
        "ret_log_decay_fwd": base_decay[None] * jnp.exp(nrm(ks[4], (DEPTH, RET_HEADS), 0.1)),
        "ret_log_decay_bwd": base_decay[None] * jnp.exp(nrm(ks[5], (DEPTH, RET_HEADS), 0.1)),
        "ret_norm": 1.0 + nrm(ks[6], (DEPTH, RV), 0.02),
        "w_ret_out": nrm(ks[7], (DEPTH, RV, D_MODEL), RV ** -0.5),
        "w_att_out": nrm(ks[8], (DEPTH, A_OUT, D_MODEL), A_OUT ** -0.5),
        "w_out": nrm(ks[9], (DEPTH, D_MODEL, D_MODEL), D_MODEL ** -0.5),
        "norm_ffn": 1.0 + nrm(ks[10], (DEPTH, D_MODEL), 0.02),
        "w_router": nrm(ks[11], (DEPTH, D_MODEL, N_EXPERTS), D_MODEL ** -0.5),
        "w_exp_gate": nrm(ks[12], (DEPTH, N_EXPERTS, D_MODEL, EXPERT_HIDDEN), D_MODEL ** -0.5),
        "w_exp_up": nrm(ks[13], (DEPTH, N_EXPERTS, D_MODEL, EXPERT_HIDDEN), D_MODEL ** -0.5),
        "w_exp_down": nrm(ks[14], (DEPTH, N_EXPERTS, EXPERT_HIDDEN, D_MODEL), EXPERT_HIDDEN ** -0.5),
        "norm_final": 1.0 + nrm(ks[15], (D_MODEL,), 0.02),
    }


def reference(x_prompt, x_sample, norm_mix, w_in, ret_log_decay_fwd, ret_log_decay_bwd, ret_norm, w_ret_out,
              w_att_out, w_out, norm_ffn, w_router, w_exp_gate, w_exp_up, w_exp_down, norm_final):
    y_prompt = encoder(x_prompt, norm_mix, w_in, ret_log_decay_fwd, ret_log_decay_bwd, ret_norm, w_ret_out,
                       w_att_out, w_out, norm_ffn, w_router, w_exp_gate, w_exp_up, w_exp_down, norm_final)
    y_sample = encoder(x_sample, norm_mix, w_in, ret_log_decay_fwd, ret_log_decay_bwd, ret_norm, w_ret_out,
                       w_att_out, w_out, norm_ffn, w_router, w_exp_gate, w_exp_up, w_exp_down, norm_final)
    return (y_prompt, y_sample)
```

```python
import functools

import numpy as np
import jax
import jax.numpy as jnp
from jax import lax
from jax.experimental import pallas as pl
from jax.experimental.pallas import tpu as pltpu

F32 = jnp.float32
BF16 = jnp.bfloat16

D_MODEL = 1024
SEQ = 4096
RET_HEADS = 4
RET_DK = 128
RET_DV = 256
RET_CHUNK = 128
ATT_DILATIONS = (1, 4, 16)
ATT_HALF_SPAN = 64
ATT_HEADS_PER_GROUP = 8
ATT_DH = 64
ATT_PAIRS = ATT_HEADS_PER_GROUP // 2
N_EXPERTS = 16
EC_CAPACITY_FACTOR = 2
EXPERT_HIDDEN = 2048
ROPE_THETA = 10000.0
EPS = 1e-6
RQ = RET_HEADS * RET_DK
RV = RET_HEADS * RET_DV
AQ = 3 * ATT_HEADS_PER_GROUP * ATT_DH
A_OUT = ATT_HEADS_PER_GROUP * ATT_DH
LANES = 128

PROJ_TM = 512
ATT_QB = 128
ATT_KW = ATT_QB + 2 * ATT_HALF_SPAN
FFN_TS = 256
FFN_CH = 256
CMB_TM = 256
CMB_CK = 128
CMB_MAXCH = CMB_TM // CMB_CK + 1
MASK_NEG = -1e30
VMEM_LIMIT = 56 * 1024 * 1024


def _params(*sem):
    return pltpu.CompilerParams(dimension_semantics=sem, vmem_limit_bytes=VMEM_LIMIT)


def _sigmoid(x):
    return 1.0 / (1.0 + jnp.exp(-x))


def _rms_scale(x):
    return x * lax.rsqrt(jnp.mean(x * x, axis=-1, keepdims=True) + EPS)


def _rope(a, cos, sin):
    return a * cos + pltpu.roll(a, 64, 1) * sin


def _proj_ret_kernel(x_ref, nm_ref, w_ref, cos_ref, sin_ref, rq_ref, rk_ref, rv_ref, rg_ref, g_ref):
    h = (_rms_scale(x_ref[...]) * nm_ref[...]).astype(BF16)
    cos = cos_ref[...]
    sin = sin_ref[...]

    def proj(c0, width):
        return jnp.dot(h, w_ref[:, c0:c0 + width], preferred_element_type=F32)

    acc = proj(0, RQ)
    for hh in range(RET_HEADS):
        sl = slice(hh * LANES, (hh + 1) * LANES)
        rq_ref[:, sl] = _rope(acc[:, sl], cos, sin).astype(BF16)
    acc = proj(RQ, RQ)
    for hh in range(RET_HEADS):
        sl = slice(hh * LANES, (hh + 1) * LANES)
        rk_ref[:, sl] = (_rope(acc[:, sl], cos, sin) * (RET_DK ** -0.5)).astype(BF16)
    for c in range(RV // 512):
        rv_ref[:, c * 512:(c + 1) * 512] = proj(2 * RQ + c * 512, 512).astype(BF16)
    for c in range(RV // 512):
        acc = proj(2 * RQ + RV + c * 512, 512)
        rg_ref[:, c * 512:(c + 1) * 512] = (acc * _sigmoid(acc)).astype(BF16)
    for c in range(2 * D_MODEL // 512):
        acc = proj(2 * RQ + 2 * RV + c * 512, 512)
        g_ref[:, c * 512:(c + 1) * 512] = _sigmoid(acc).astype(BF16)


def _proj_ret(x2d, nm, w, cos, sin):
    n = x2d.shape[0]
    nt = SEQ // PROJ_TM
    tm = PROJ_TM
    row = lambda i: (i, 0)
    fixed = lambda i: (0, 0)
    tab = lambda i: (i % nt, 0)
    return pl.pallas_call(
        _proj_ret_kernel,
        grid=(n // tm,),
        in_specs=[pl.BlockSpec((tm, D_MODEL), row), pl.BlockSpec((1, D_MODEL), fixed),
                  pl.BlockSpec(w.shape, fixed), pl.BlockSpec((tm, LANES), tab), pl.BlockSpec((tm, LANES), tab)],
        out_specs=[pl.BlockSpec((tm, RQ), row), pl.BlockSpec((tm, RQ), row), pl.BlockSpec((tm, RV), row),
                   pl.BlockSpec((tm, RV), row), pl.BlockSpec((tm, 2 * D_MODEL), row)],
        out_shape=[jax.ShapeDtypeStruct((n, RQ), BF16), jax.ShapeDtypeStruct((n, RQ), BF16),
                   jax.ShapeDtypeStruct((n, RV), BF16), jax.ShapeDtypeStruct((n, RV), BF16),
                   jax.ShapeDtypeStruct((n, 2 * D_MODEL), BF16)],
        compiler_params=_params("parallel"),
        name="proj_ret",
    )(x2d, nm, w, cos, sin)


def _proj_att_kernel(x_ref, nm_ref, w_ref, c0_ref, s0_ref, c1_ref, s1_ref, c2_ref, s2_ref, *rest):
    out_refs = rest[:9]
    scr_ref = rest[9]
    h = (_rms_scale(x_ref[...]) * nm_ref[...]).astype(BF16)
    tabs = ((c0_ref, s0_ref), (c1_ref, s1_ref), (c2_ref, s2_ref))
    tm = x_ref.shape[0]
    for sec in range(3):
        for g, dil in enumerate(ATT_DILATIONS):
            c0 = (sec * 3 + g) * A_OUT
            acc = jnp.dot(h, w_ref[:, c0:c0 + A_OUT], preferred_element_type=F32)
            out_ref = out_refs[sec * 3 + g]
            if dil > 1:
                for p in range(ATT_PAIRS):
                    scr_ref[p] = acc[:, p * LANES:(p + 1) * LANES]
            for r in range(dil):
                for p in range(ATT_PAIRS):
                    if dil == 1:
                        a = acc[:, p * LANES:(p + 1) * LANES]
                    else:
                        a = scr_ref[p, pl.ds(r, tm // dil, stride=dil), :]
                    if sec < 2:
                        a = _rope(a, tabs[g][0][r], tabs[g][1][r])
                    out_ref[0, p, r] = a.astype(BF16)


def _proj_att(x2d, nm, w, tabs, batch):
    n = x2d.shape[0]
    tm = PROJ_TM
    nt = SEQ // tm
    row = lambda i: (i, 0)
    fixed = lambda i: (0, 0)
    in_specs = [pl.BlockSpec((tm, D_MODEL), row), pl.BlockSpec((1, D_MODEL), fixed), pl.BlockSpec(w.shape, fixed)]
    for dil in ATT_DILATIONS:
        for _ in range(2):
            in_specs.append(pl.BlockSpec((dil, tm // dil, LANES), lambda i: (0, i % nt, 0)))
    out_specs, out_shape = [], []
    for _ in range(3):
        for dil in ATT_DILATIONS:
            out_specs.append(pl.BlockSpec((1, ATT_PAIRS, dil, tm // dil, LANES),
                                          lambda i: (i // nt, 0, 0, i % nt, 0)))
            out_shape.append(jax.ShapeDtypeStruct((batch, ATT_PAIRS, dil, SEQ // dil, LANES), BF16))
    return pl.pallas_call(
        _proj_att_kernel,
        grid=(n // tm,),
        in_specs=in_specs,
        out_specs=out_specs,
        out_shape=out_shape,
        scratch_shapes=[pltpu.VMEM((ATT_PAIRS, tm, LANES), F32)],
        compiler_params=_params("parallel"),
        name="proj_att",
    )(x2d, nm, w, *tabs)


def _retention_kernel(lgf_ref, lgb_ref, q_ref, k_ref, v_ref, rg_ref, nrm_ref, o_ref, cross_ref):
    hd = pl.program_id(1)
    lgf = lgf_ref[hd]
    lgb = lgb_ref[hd]
    c = RET_CHUNK
    nc = q_ref.shape[0] // c
    ti = lax.broadcasted_iota(jnp.int32, (c, c), 0)
    si = lax.broadcasted_iota(jnp.int32, (c, c), 1)
    diff = (ti - si).astype(F32)
    fwd = diff >= 0.0
    decay = jnp.where(fwd, jnp.exp(lgf * jnp.where(fwd, diff, 0.0)), jnp.exp(lgb * jnp.where(fwd, 0.0, -diff)))
    idx = lax.broadcasted_iota(jnp.int32, (c, 1), 0).astype(F32)
    qf_dec = jnp.exp(lgf * (idx + 1.0))
    kf_dec = jnp.exp(lgf * (c - 1.0 - idx))
    qb_dec = jnp.exp(lgb * (c - idx))
    kb_dec = jnp.exp(lgb * idx)
    one = jnp.ones((1, 1), F32)
    cdf = jnp.exp(one * (lgf * c))
    cdb = jnp.exp(one * (lgb * c))
    nrm = nrm_ref[...]
    tn = (((0,), (0,)), ((), ()))
    nt = (((1,), (1,)), ((), ()))

    def bwd_step(i, state):
        rows = pl.ds(pl.multiple_of((nc - 1 - i) * c, c), c)
        q = q_ref[rows, :].astype(F32)
        k = k_ref[rows, :].astype(F32)
        cross_ref[rows, :] = jnp.dot((q * qb_dec).astype(BF16), state.astype(BF16), preferred_element_type=F32)
        kv = lax.dot_general((k * kb_dec).astype(BF16), v_ref[rows, :], tn, preferred_element_type=F32)
        return state * cdb + kv

    lax.fori_loop(0, nc, bwd_step, jnp.zeros((RET_DK, RET_DV), F32))

    def fwd_step(i, state):
        rows = pl.ds(pl.multiple_of(i * c, c), c)
        qb = q_ref[rows, :]
        kb = k_ref[rows, :]
        v = v_ref[rows, :]
        scores = lax.dot_general(qb, kb, nt, preferred_element_type=F32) * decay
        out = jnp.dot(scores.astype(BF16), v, preferred_element_type=F32)
        out += jnp.dot((qb.astype(F32) * qf_dec).astype(BF16), state.astype(BF16), preferred_element_type=F32)
        out += cross_ref[rows, :]
        gate = rg_ref[rows, :].astype(F32)
        o_ref[rows, :] = (gate * (_rms_scale(out) * nrm)).astype(BF16)
        kv = lax.dot_general((kb.astype(F32) * kf_dec).astype(BF16), v, tn, preferred_element_type=F32)
        return state * cdf + kv

    lax.fori_loop(0, nc, fwd_step, jnp.zeros((RET_DK, RET_DV), F32))


def _retention(lgf, lgb, rq, rk, rv, rg, nrm, batch):
    n = rq.shape[0]
    bh = lambda b, h, *_: (b, h)
    return pl.pallas_call(
        _retention_kernel,
        grid_spec=pltpu.PrefetchScalarGridSpec(
            num_scalar_prefetch=2,
            grid=(batch, RET_HEADS),
            in_specs=[pl.BlockSpec((SEQ, RET_DK), bh), pl.BlockSpec((SEQ, RET_DK), bh),
                      pl.BlockSpec((SEQ, RET_DV), bh), pl.BlockSpec((SEQ, RET_DV), bh),
                      pl.BlockSpec((1, RET_DV), lambda b, h, *_: (0, h))],
            out_specs=pl.BlockSpec((SEQ, RET_DV), bh),
            scratch_shapes=[pltpu.VMEM((SEQ, RET_DV), F32)],
        ),
        out_shape=jax.ShapeDtypeStruct((n, RV), BF16),
        compiler_params=_params("parallel", "parallel"),
        name="retention",
    )(lgf, lgb, rq, rk, rv, rg, nrm)


def _attn_block(q, kw, vw, bias):
    lane = lax.broadcasted_iota(jnp.int32, q.shape, 1)
    head0_qk = (lane < 32) | ((lane >= 64) & (lane < 96))
    outs, lses = [], []
    for hh in range(2):
        qm = jnp.where(head0_qk if hh == 0 else jnp.logical_not(head0_qk), q, jnp.zeros_like(q))
        s = lax.dot_general(qm, kw, (((1,), (1,)), ((), ())), preferred_element_type=F32) + bias
        m = jnp.max(s, axis=-1, keepdims=True)
        p = jnp.exp(s - m)
        l = jnp.sum(p, axis=-1, keepdims=True)
        o = jnp.dot(p.astype(BF16), vw, preferred_element_type=F32)
        outs.append(o / l)
        lses.append(m + jnp.log(l))
    head0_v = lane < ATT_DH
    return jnp.where(head0_v, outs[0], outs[1]), jnp.where(head0_v, lses[0], lses[1])


def _attention_kernel(q0, k0, v0, q1, k1, v1, q2, k2, v2, o_ref, o1_ref, l1_ref, o2_ref, l2_ref):
    qb, kw = ATT_QB, ATT_KW
    ji = lax.broadcasted_iota(jnp.int32, (qb, kw), 1) - lax.broadcasted_iota(jnp.int32, (qb, kw), 0)

    def group(q_ref, k_ref, v_ref, dil, finish):
        sub_len = SEQ // dil
        nblk = sub_len // qb

        def body(it, carry):
            r = it // nblk
            l0 = (it % nblk) * qb
            kstart = pl.multiple_of(jnp.clip(l0 - ATT_HALF_SPAN, 0, sub_len - kw), ATT_HALF_SPAN)
            off = l0 - kstart
            rel = ji - off
            bias = jnp.where((rel >= -ATT_HALF_SPAN) & (rel <= ATT_HALF_SPAN), 0.0, MASK_NEG).astype(F32)
            q = q_ref[0, 0, r, pl.ds(pl.multiple_of(l0, qb), qb), :]
            kwin = k_ref[0, 0, r, pl.ds(kstart, kw), :]
            vwin = v_ref[0, 0, r, pl.ds(kstart, kw), :]
            o, lse = _attn_block(q, kwin, vwin, bias)
            finish(l0, r, o, lse)
            return carry

        lax.fori_loop(0, dil * nblk, body, 0)

    def store_strided(o_nat, l_nat, dil):
        def finish(l0, r, o, lse):
            rows = pl.ds(l0 * dil + r, ATT_QB, stride=dil)
            o_nat[rows, :] = o
            l_nat[rows, :] = lse
        return finish

    group(q2, k2, v2, ATT_DILATIONS[2], store_strided(o2_ref, l2_ref, ATT_DILATIONS[2]))
    group(q1, k1, v1, ATT_DILATIONS[1], store_strided(o1_ref, l1_ref, ATT_DILATIONS[1]))

    def merge(l0, r, o, lse):
        rows = pl.ds(pl.multiple_of(l0, ATT_QB), ATT_QB)
        la, lb = l1_ref[rows, :], l2_ref[rows, :]
        mx = jnp.maximum(lse, jnp.maximum(la, lb))
        w0, w1, w2 = jnp.exp(lse - mx), jnp.exp(la - mx), jnp.exp(lb - mx)
        num = w0 * o + w1 * o1_ref[rows, :] + w2 * o2_ref[rows, :]
        o_ref[rows, :] = (num / (w0 + w1 + w2)).astype(BF16)

    group(q0, k0, v0, ATT_DILATIONS[0], merge)


def _attention(qkv, batch):
    in_specs = []
    for g, dil in enumerate(ATT_DILATIONS):
        for _ in range(3):
            in_specs.append(pl.BlockSpec((1, 1, dil, SEQ // dil, LANES), lambda b, p: (b, p, 0, 0, 0)))
    args = []
    for g in range(3):
        args += [qkv[0 * 3 + g], qkv[1 * 3 + g], qkv[2 * 3 + g]]
    return pl.pallas_call(
        _attention_kernel,
        grid=(batch, ATT_PAIRS),
        in_specs=in_specs,
        out_specs=pl.BlockSpec((SEQ, LANES), lambda b, p: (b, p)),
        out_shape=jax.ShapeDtypeStruct((batch * SEQ, A_OUT), BF16),
        scratch_shapes=[pltpu.VMEM((SEQ, LANES), F32) for _ in range(4)],
        compiler_params=_params("parallel", "parallel"),
        name="dilated_attention",
    )(*args)


def _mix_out_kernel(a_ref, att_ref, g_ref, x_ref, wr_ref, wa_ref, wo_ref, nf_ref, rhi_ref, rlo_ref,
                    x1_ref, h2_ref, aff_ref):
    bra = jnp.dot(a_ref[...], wr_ref[...], preferred_element_type=F32)
    brb = jnp.dot(att_ref[...], wa_ref[...], preferred_element_type=F32)
    g = g_ref[...].astype(F32)
    merged = g[:, :D_MODEL] * bra + g[:, D_MODEL:] * brb
    x1 = x_ref[...] + jnp.dot(merged.astype(BF16), wo_ref[...], preferred_element_type=F32)
    x1_ref[...] = x1
    h2 = _rms_scale(x1) * nf_ref[...]
    hi = h2.astype(BF16)
    lo = (h2 - hi.astype(F32)).astype(BF16)
    h2_ref[...] = hi
    nt = (((1,), (1,)), ((), ()))
    logits = (lax.dot_general(rhi_ref[...], hi, nt, preferred_element_type=F32)
              + lax.dot_general(rhi_ref[...], lo, nt, preferred_element_type=F32)
              + lax.dot_general(rlo_ref[...], hi, nt, preferred_element_type=F32))
    e = jnp.exp(logits - jnp.max(logits, axis=0, keepdims=True))
    aff_ref[...] = e / jnp.sum(e, axis=0, keepdims=True)


def _mix_out(a, att, g, x2d, wr, wa, wo, nf, rhi, rlo):
    n = x2d.shape[0]
    tm = PROJ_TM
    row = lambda i: (i, 0)
    fixed = lambda i: (0, 0)
    return pl.pallas_call(
        _mix_out_kernel,
        grid=(n // tm,),
        in_specs=[pl.BlockSpec((tm, RV), row), pl.BlockSpec((tm, A_OUT), row), pl.BlockSpec((tm, 2 * D_MODEL), row),
                  pl.BlockSpec((tm, D_MODEL), row), pl.BlockSpec(wr.shape, fixed), pl.BlockSpec(wa.shape, fixed),
                  pl.BlockSpec(wo.shape, fixed), pl.BlockSpec((1, D_MODEL), fixed),
                  pl.BlockSpec(rhi.shape, fixed), pl.BlockSpec(rlo.shape, fixed)],
        out_specs=[pl.BlockSpec((tm, D_MODEL), row), pl.BlockSpec((tm, D_MODEL), row),
                   pl.BlockSpec((N_EXPERTS, tm), lambda i: (0, i))],
        out_shape=[jax.ShapeDtypeStruct((n, D_MODEL), F32), jax.ShapeDtypeStruct((n, D_MODEL), BF16),
                   jax.ShapeDtypeStruct((N_EXPERTS, n), F32)],
        compiler_params=_params("parallel"),
        name="mix_out",
    )(a, att, g, x2d, wr, wa, wo, nf, rhi, rlo)


def _topk_kernel(aff_ref, spos_ref, offs_ref, rng_ref, *, cap):
    nchunk = aff_ref.shape[1]
    n_tiles = cap // FFN_TS
    li = lax.broadcasted_iota(jnp.int32, (LANES, LANES), 0)
    lj = lax.broadcasted_iota(jnp.int32, (LANES, LANES), 1)
    upper = jnp.where(li <= lj, 1.0, 0.0).astype(BF16)
    ci = lax.broadcasted_iota(jnp.int32, (nchunk, nchunk), 0)
    cj = lax.broadcasted_iota(jnp.int32, (nchunk, nchunk), 1)
    lower = jnp.where(cj < ci, 1.0, 0.0).astype(BF16)
    lane8 = lax.broadcasted_iota(jnp.int32, (8, LANES), 1)
    sub8 = lax.broadcasted_iota(jnp.int32, (8, LANES), 0)

    def total(x):
        return jnp.sum(jnp.sum(x, axis=0, keepdims=True), axis=1, keepdims=True)

    def cumsum(mask):
        within = jnp.dot(mask.astype(BF16), upper, preferred_element_type=F32)
        tot = jnp.broadcast_to(within[:, LANES - 1:LANES], within.shape).astype(BF16)
        before = jnp.dot(lower, tot, preferred_element_type=F32)
        return within + before, before

    def per_expert(e, carry):
        bits = pltpu.bitcast(aff_ref[e], jnp.int32)

        def search(i, thr):
            cand = thr | (1 << (30 - i))
            cnt = total(jnp.where(bits >= cand, 1, 0))
            return jnp.where(cnt >= cap, cand, thr)

        thr = lax.fori_loop(0, 31, search, jnp.zeros((1, 1), jnp.int32))
        gt = bits > thr
        eq = bits == thr
        need = (cap - total(jnp.where(gt, 1, 0))).astype(F32)
        eq_f = jnp.where(eq, 1.0, 0.0)
        eq_cum, _ = cumsum(eq_f)
        sel = gt | (eq & (eq_cum - eq_f < need))
        cum, before = cumsum(jnp.where(sel, 1.0, 0.0))
        spos_ref[e] = jnp.where(sel, cum - 1.0, -1.0).astype(jnp.int32)
        offs_ref[e] = before.astype(jnp.int32)
        first = jnp.zeros((8, LANES), jnp.int32)
        for s in range(n_tiles):
            a = total(jnp.where(cum <= float(s * FFN_TS), 1, 0))
            b = total(jnp.where(cum < float((s + 1) * FFN_TS), 1, 0))
            first = jnp.where((lane8 == s) & (sub8 == 0), a, first)
            first = jnp.where((lane8 == s) & (sub8 == 1), b, first)
        rng_ref[e] = first
        return carry

    lax.fori_loop(0, N_EXPERTS, per_expert, 0)


def _topk(aff3, cap):
    e, nchunk, _ = aff3.shape
    return pl.pallas_call(
        functools.partial(_topk_kernel, cap=cap),
        out_shape=[jax.ShapeDtypeStruct((e, nchunk, LANES), jnp.int32),
                   jax.ShapeDtypeStruct((e, nchunk, LANES), jnp.int32),
                   jax.ShapeDtypeStruct((e, 8, LANES), jnp.int32)],
        compiler_params=pltpu.CompilerParams(vmem_limit_bytes=VMEM_LIMIT),
        name="expert_choice_topk",
    )(aff3)


def _ffn_kernel(rng_ref, spos_ref, h2_hbm, wg_ref, wu_ref, wd_ref, out_ref, buf_ref, sem_ref, xs_ref):
    e = pl.program_id(0)
    s = pl.program_id(1)
    n_tiles = pl.num_programs(1)
    ts, ch = FFN_TS, FFN_CH
    c0 = rng_ref[(e * n_tiles + s) * 2]
    c1 = rng_ref[(e * n_tiles + s) * 2 + 1]

    def chunk_copy(c, slot):
        return pltpu.make_async_copy(h2_hbm.at[pl.ds(pl.multiple_of(c * ch, ch), ch), :], buf_ref.at[slot],
                                     sem_ref.at[slot])

    chunk_copy(c0, 0).start()
    xs_ref[...] = jnp.zeros_like(xs_ref)
    slot_ids = s * ts + lax.broadcasted_iota(jnp.int32, (ts, ch), 0)

    def body(c, carry):
        slot = (c - c0) % 2
        chunk_copy(c, slot).wait()

        @pl.when(c < c1)
        def _():
            chunk_copy(c + 1, 1 - slot).start()

        pos = spos_ref[0, pl.ds(c, 1), :]
        onehot = jnp.where(slot_ids == pos, 1.0, 0.0).astype(BF16)
        xs_ref[...] += jnp.dot(onehot, buf_ref[slot], preferred_element_type=F32)
        return carry

    lax.fori_loop(c0, c1 + 1, body, 0)
    xs = xs_ref[...].astype(BF16)
    gate = jnp.dot(xs, wg_ref[0], preferred_element_type=F32)
    up = jnp.dot(xs, wu_ref[0], preferred_element_type=F32)
    hid = (gate * _sigmoid(gate) * up).astype(BF16)
    out_ref[0] = jnp.dot(hid, wd_ref[0], preferred_element_type=F32).astype(BF16)


def _ffn(rng, spos_ch, h2, wg, wu, wd, cap):
    n_tiles = cap // FFN_TS
    nch = spos_ch.shape[1]
    wmap = lambda e, s, *_: (e, 0, 0)
    return pl.pallas_call(
        _ffn_kernel,
        grid_spec=pltpu.PrefetchScalarGridSpec(
            num_scalar_prefetch=1,
            grid=(N_EXPERTS, n_tiles),
            in_specs=[pl.BlockSpec((1, nch, FFN_CH), wmap),
                      pl.BlockSpec(memory_space=pl.ANY),
                      pl.BlockSpec((1, D_MODEL, EXPERT_HIDDEN), wmap),
                      pl.BlockSpec((1, D_MODEL, EXPERT_HIDDEN), wmap),
                      pl.BlockSpec((1, EXPERT_HIDDEN, D_MODEL), wmap)],
            out_specs=pl.BlockSpec((1, FFN_TS, D_MODEL), lambda e, s, *_: (e, s, 0)),
            scratch_shapes=[pltpu.VMEM((2, FFN_CH, D_MODEL), BF16), pltpu.SemaphoreType.DMA((2,)),
                            pltpu.VMEM((FFN_TS, D_MODEL), F32)],
        ),
        out_shape=jax.ShapeDtypeStruct((N_EXPERTS, cap, D_MODEL), BF16),
        compiler_params=_params("arbitrary", "arbitrary"),
        name="expert_ffn",
    )(rng, spos_ch, h2, wg, wu, wd)


def _combine_kernel(base_ref, spos_ref, aff_ref, x1_ref, nf_ref, eo_hbm, o_ref, stage_ref, sem_ref, acc_ref):
    i = pl.program_id(0)
    tm, ck = CMB_TM, CMB_CK
    lane_ids = lax.broadcasted_iota(jnp.int32, (tm, ck), 1)

    def chunk_copy(e, j, k):
        return pltpu.make_async_copy(eo_hbm.at[e, pl.ds(pl.multiple_of(j * ck, ck), ck), :], stage_ref.at[e, k],
                                     sem_ref.at[e, k])

    plan = []
    for e in range(N_EXPERTS):
        b0 = base_ref[i * N_EXPERTS + e]
        b1 = base_ref[(i + 1) * N_EXPERTS + e]
        j0 = b0 // ck
        j1 = (b1 - 1) // ck
        for k in range(CMB_MAXCH):
            plan.append((e, k, j0 + k, (b1 > b0) & (j0 + k <= j1)))

    for e, k, j, need in plan:
        @pl.when(need)
        def _(e=e, k=k, j=j):
            chunk_copy(e, j, k).start()

    acc_ref[...] = x1_ref[...]
    for e, k, j, need in plan:
        @pl.when(need)
        def _(e=e, k=k, j=j):
            chunk_copy(e, j, k).wait()
            onehot = jnp.where(spos_ref[:, e:e + 1] == j * ck + lane_ids, 1.0, 0.0).astype(BF16)
            z = jnp.dot(onehot, stage_ref[e, k], preferred_element_type=F32)
            acc_ref[...] += aff_ref[:, e:e + 1] * z

    o_ref[...] = _rms_scale(acc_ref[...]) * nf_ref[...]


def _combine(base, spos_t, aff_n, x1, nf, eo):
    n = x1.shape[0]
    tm = CMB_TM
    row = lambda i, *_: (i, 0)
    return pl.pallas_call(
        _combine_kernel,
        grid_spec=pltpu.PrefetchScalarGridSpec(
            num_scalar_prefetch=1,
            grid=(n // tm,),
            in_specs=[pl.BlockSpec((tm, N_EXPERTS), row), pl.BlockSpec((tm, N_EXPERTS), row),
                      pl.BlockSpec((tm, D_MODEL), row), pl.BlockSpec((1, D_MODEL), lambda i, *_: (0, 0)),
                      pl.BlockSpec(memory_space=pl.ANY)],
            out_specs=pl.BlockSpec((tm, D_MODEL), row),
            scratch_shapes=[pltpu.VMEM((N_EXPERTS, CMB_MAXCH, CMB_CK, D_MODEL), BF16),
                            pltpu.SemaphoreType.DMA((N_EXPERTS, CMB_MAXCH)),
                            pltpu.VMEM((tm, D_MODEL), F32)],
        ),
        out_shape=jax.ShapeDtypeStruct((n, D_MODEL), F32),
        compiler_params=_params("arbitrary"),
        name="combine_final",
    )(base, spos_t, aff_n, x1, nf, eo)


def _moe(x1, h2, aff_t, p):
    n = x1.shape[0]
    cap = EC_CAPACITY_FACTOR * n // N_EXPERTS
    spos, offs, rng = _topk(aff_t.reshape(N_EXPERTS, n // LANES, LANES), cap)
    n_tiles = cap // FFN_TS
    rng = (rng[:, :2, :n_tiles] // FFN_CH).transpose(0, 2, 1).reshape(-1)
    eo = _ffn(rng, spos.reshape(N_EXPERTS, n // FFN_CH, FFN_CH), h2,
              p["w_exp_gate"], p["w_exp_up"], p["w_exp_down"], cap)
    base = offs[:, ::CMB_TM // LANES, 0].T
    base = jnp.concatenate([base, jnp.full((1, N_EXPERTS), cap, jnp.int32)], axis=0).reshape(-1)
    return _combine(base, spos.reshape(N_EXPERTS, n).T, aff_t.T, x1, p["norm_final"], eo)


def _pair_perm():
    base = np.concatenate([np.arange(0, 32), np.arange(64, 96), np.arange(32, 64), np.arange(96, 128)])
    return np.concatenate([base + LANES * b for b in range(AQ // LANES)])


def _rope_tables():
    pos = jnp.arange(SEQ, dtype=F32)
    ret_inv = 1.0 / (ROPE_THETA ** jnp.linspace(0.0, 1.0, RET_DK // 2, dtype=F32))
    ang = pos[:, None] * ret_inv[None, :]
    ret_cos = jnp.concatenate([jnp.cos(ang), jnp.cos(ang)], axis=-1)
    ret_sin = jnp.concatenate([-jnp.sin(ang), jnp.sin(ang)], axis=-1)
    att_inv = ROPE_THETA ** (-jnp.arange(0, ATT_DH, 2, dtype=F32) / ATT_DH)
    ang = pos[:, None] * att_inv[None, :]
    c, s = jnp.cos(ang), jnp.sin(ang)
    att_cos = jnp.concatenate([c, c, c, c], axis=-1)
    att_sin = jnp.concatenate([-s, -s, s, s], axis=-1)
    att_tabs = []
    for dil in ATT_DILATIONS:
        for tab in (att_cos, att_sin):
            att_tabs.append(tab.reshape(SEQ // dil, dil, LANES).transpose(1, 0, 2))
    return ret_cos, ret_sin, att_tabs


def _encoder(x, p):
    batch = x.shape[0]
    x2d = x.reshape(batch * SEQ, D_MODEL)
    rq, rk, rv, rg, gates = _proj_ret(x2d, p["norm_mix"], p["w_ret"], p["ret_cos"], p["ret_sin"])
    qkv = _proj_att(x2d, p["norm_mix"], p["w_att"], p["att_tabs"], batch)
    a = _retention(p["lgf"], p["lgb"], rq, rk, rv, rg, p["ret_norm"], batch)
    att = _attention(qkv, batch)
    x1, h2, aff_t = _mix_out(a, att, gates, x2d, p["w_ret_out"], p["w_att_out"], p["w_out"], p["norm_ffn"],
                             p["router_hi"], p["router_lo"])
    return _moe(x1, h2, aff_t, p).reshape(batch, SEQ, D_MODEL)


def kernel(x_prompt, x_sample, norm_mix, w_in, ret_log_decay_fwd, ret_log_decay_bwd, ret_norm, w_ret_out,
           w_att_out, w_out, norm_ffn, w_router, w_exp_gate, w_exp_up, w_exp_down, norm_final):
    w = w_in[0]
    c_rv = 2 * RQ
    c_rg = c_rv + RV
    c_aq = c_rg + RV
    c_ak = c_aq + AQ
    c_av = c_ak + AQ
    c_gate = c_av + AQ
    perm = _pair_perm()
    w_aq = w[:, c_aq:c_ak][:, perm] * (ATT_DH ** -0.5)
    w_ak = w[:, c_ak:c_av][:, perm]
    w_av = w[:, c_av:c_gate]
    ret_cos, ret_sin, att_tabs = _rope_tables()
    router_t = w_router[0].T
    router_hi = router_t.astype(BF16)
    p = {
        "norm_mix": norm_mix,
        "w_ret": jnp.concatenate([w[:, :c_aq], w[:, c_gate:]], axis=1).astype(BF16),
        "w_att": jnp.concatenate([w_aq, w_ak, w_av], axis=1).astype(BF16),
        "ret_cos": ret_cos, "ret_sin": ret_sin, "att_tabs": att_tabs,
        "lgf": ret_log_decay_fwd[0], "lgb": ret_log_decay_bwd[0],
        "ret_norm": ret_norm,
        "w_ret_out": w_ret_out[0].astype(BF16), "w_att_out": w_att_out[0].astype(BF16),
        "w_out": w_out[0].astype(BF16),
        "norm_ffn": norm_ffn,
        "router_hi": router_hi, "router_lo": (router_t - router_hi.astype(F32)).astype(BF16),
        "w_exp_gate": w_exp_gate[0].astype(BF16), "w_exp_up": w_exp_up[0].astype(BF16),
        "w_exp_down": w_exp_down[0].astype(BF16),
        "norm_final": norm_final.reshape(1, D_MODEL),
    }
    return _encoder(x_prompt, p), _encoder(x_sample, p)
```

```python
import functools

import numpy as np
import jax
import jax.numpy as jnp
from jax import lax
from jax.experimental import pallas as pl
from jax.experimental.pallas import tpu as pltpu

F32 = jnp.float32
BF16 = jnp.bfloat16

D_MODEL = 1024
SEQ = 4096
RET_HEADS = 4
RET_DK = 128
RET_DV = 256
RET_CHUNK = 128
RET_UNROLL = 4
RET_FIN = 512
ATT_DILATIONS = (1, 4, 16)
ATT_HALF_SPAN = 64
ATT_HEADS_PER_GROUP = 8
ATT_DH = 64
ATT_PAIRS = ATT_HEADS_PER_GROUP // 2
N_EXPERTS = 16
EC_CAPACITY_FACTOR = 2
EXPERT_HIDDEN = 2048
ROPE_THETA = 10000.0
EPS = 1e-6
RQ = RET_HEADS * RET_DK
RV = RET_HEADS * RET_DV
AQ = 3 * ATT_HEADS_PER_GROUP * ATT_DH
A_OUT = ATT_HEADS_PER_GROUP * ATT_DH
LANES = 128

PROJ_TM = 512
ATT_QB = 128
ATT_KW = ATT_QB + 2 * ATT_HALF_SPAN
ATT_UNROLL = 8
FFN_TS = 256
FFN_CH = 512
FFN_RING = 6
CMB_TM = 512
CMB_WIN = 128
CMB_ALIGN = 16
MASK_NEG = -1e30
VMEM_LIMIT = 56 * 1024 * 1024


def _params(*sem):
    return pltpu.CompilerParams(dimension_semantics=sem, vmem_limit_bytes=VMEM_LIMIT)


def _sigmoid(x):
    return 1.0 / (1.0 + jnp.exp(-x))


def _rms_scale(x):
    return x * lax.rsqrt(jnp.mean(x * x, axis=-1, keepdims=True) + EPS)


def _rope(a, cos, sin):
    return a * cos + pltpu.roll(a, 64, 1) * sin


def _proj_ret_kernel(x_ref, nm_ref, w_ref, cos_ref, sin_ref, rq_ref, rk_ref, rv_ref, rg_ref, g_ref):
    h = (_rms_scale(x_ref[...]) * nm_ref[...]).astype(BF16)
    cos = cos_ref[...]
    sin = sin_ref[...]

    def proj(c0, width):
        return jnp.dot(h, w_ref[:, c0:c0 + width], preferred_element_type=F32)

    acc = proj(0, RQ)
    for hh in range(RET_HEADS):
        sl = slice(hh * LANES, (hh + 1) * LANES)
        rq_ref[:, sl] = _rope(acc[:, sl], cos, sin).astype(BF16)
    acc = proj(RQ, RQ)
    for hh in range(RET_HEADS):
        sl = slice(hh * LANES, (hh + 1) * LANES)
        rk_ref[:, sl] = (_rope(acc[:, sl], cos, sin) * (RET_DK ** -0.5)).astype(BF16)
    for c in range(RV // 512):
        rv_ref[:, c * 512:(c + 1) * 512] = proj(2 * RQ + c * 512, 512).astype(BF16)
    for c in range(RV // 512):
        acc = proj(2 * RQ + RV + c * 512, 512)
        rg_ref[:, c * 512:(c + 1) * 512] = (acc * _sigmoid(acc)).astype(BF16)
    for c in range(2 * D_MODEL // 512):
        acc = proj(2 * RQ + 2 * RV + c * 512, 512)
        g_ref[:, c * 512:(c + 1) * 512] = _sigmoid(acc).astype(BF16)


def _proj_ret(x2d, nm, w, cos, sin):
    n = x2d.shape[0]
    nt = SEQ // PROJ_TM
    tm = PROJ_TM
    row = lambda i: (i, 0)
    fixed = lambda i: (0, 0)
    tab = lambda i: (i % nt, 0)
    return pl.pallas_call(
        _proj_ret_kernel,
        grid=(n // tm,),
        in_specs=[pl.BlockSpec((tm, D_MODEL), row), pl.BlockSpec((1, D_MODEL), fixed),
                  pl.BlockSpec(w.shape, fixed), pl.BlockSpec((tm, LANES), tab), pl.BlockSpec((tm, LANES), tab)],
        out_specs=[pl.BlockSpec((tm, RQ), row), pl.BlockSpec((tm, RQ), row), pl.BlockSpec((tm, RV), row),
                   pl.BlockSpec((tm, RV), row), pl.BlockSpec((tm, 2 * D_MODEL), row)],
        out_shape=[jax.ShapeDtypeStruct((n, RQ), BF16), jax.ShapeDtypeStruct((n, RQ), BF16),
                   jax.ShapeDtypeStruct((n, RV), BF16), jax.ShapeDtypeStruct((n, RV), BF16),
                   jax.ShapeDtypeStruct((n, 2 * D_MODEL), BF16)],
        compiler_params=_params("parallel"),
        name="proj_ret",
    )(x2d, nm, w, cos, sin)


def _proj_att_kernel(x_ref, nm_ref, w_ref, c0_ref, s0_ref, c1_ref, s1_ref, c2_ref, s2_ref, *rest):
    out_refs = rest[:9]
    scr_ref = rest[9]
    h = (_rms_scale(x_ref[...]) * nm_ref[...]).astype(BF16)
    tabs = ((c0_ref, s0_ref), (c1_ref, s1_ref), (c2_ref, s2_ref))
    tm = x_ref.shape[0]
    for sec in range(3):
        for g, dil in enumerate(ATT_DILATIONS):
            c0 = (sec * 3 + g) * A_OUT
            acc = jnp.dot(h, w_ref[:, c0:c0 + A_OUT], preferred_element_type=F32)
            out_ref = out_refs[sec * 3 + g]
            if dil > 1:
                for p in range(ATT_PAIRS):
                    scr_ref[p] = acc[:, p * LANES:(p + 1) * LANES]
            for r in range(dil):
                for p in range(ATT_PAIRS):
                    if dil == 1:
                        a = acc[:, p * LANES:(p + 1) * LANES]
                    else:
                        a = scr_ref[p, pl.ds(r, tm // dil, stride=dil), :]
                    if sec < 2:
                        a = _rope(a, tabs[g][0][r], tabs[g][1][r])
                    out_ref[0, p, r] = a.astype(BF16)


def _proj_att(x2d, nm, w, tabs, batch):
    n = x2d.shape[0]
    tm = PROJ_TM
    nt = SEQ // tm
    row = lambda i: (i, 0)
    fixed = lambda i: (0, 0)
    in_specs = [pl.BlockSpec((tm, D_MODEL), row), pl.BlockSpec((1, D_MODEL), fixed), pl.BlockSpec(w.shape, fixed)]
    for dil in ATT_DILATIONS:
        for _ in range(2):
            in_specs.append(pl.BlockSpec((dil, tm // dil, LANES), lambda i: (0, i % nt, 0)))
    out_specs, out_shape = [], []
    for _ in range(3):
        for dil in ATT_DILATIONS:
            out_specs.append(pl.BlockSpec((1, ATT_PAIRS, dil, tm // dil, LANES),
                                          lambda i: (i // nt, 0, 0, i % nt, 0)))
            out_shape.append(jax.ShapeDtypeStruct((batch, ATT_PAIRS, dil, SEQ // dil, LANES), BF16))
    return pl.pallas_call(
        _proj_att_kernel,
        grid=(n // tm,),
        in_specs=in_specs,
        out_specs=out_specs,
        out_shape=out_shape,
        scratch_shapes=[pltpu.VMEM((ATT_PAIRS, tm, LANES), F32)],
        compiler_params=_params("parallel"),
        name="proj_att",
    )(x2d, nm, w, *tabs)


def _retention_kernel(lgf_ref, lgb_ref, q_ref, k_ref, v_ref, rg_ref, nrm_ref, o_ref, part_ref, cross_ref):
    hd = pl.program_id(1)
    lgf = lgf_ref[hd]
    lgb = lgb_ref[hd]
    c = RET_CHUNK
    nc = q_ref.shape[0] // c
    ti = lax.broadcasted_iota(jnp.int32, (c, c), 0)
    si = lax.broadcasted_iota(jnp.int32, (c, c), 1)
    diff = (ti - si).astype(F32)
    fwd = diff >= 0.0
    decay = jnp.where(fwd, jnp.exp(lgf * jnp.where(fwd, diff, 0.0)), jnp.exp(lgb * jnp.where(fwd, 0.0, -diff)))
    idx = lax.broadcasted_iota(jnp.int32, (c, 1), 0).astype(F32)
    qf_dec = jnp.exp(lgf * (idx + 1.0))
    kf_dec = jnp.exp(lgf * (c - 1.0 - idx))
    qb_dec = jnp.exp(lgb * (c - idx))
    kb_dec = jnp.exp(lgb * idx)
    one = jnp.ones((1, 1), F32)
    cdf = jnp.exp(one * (lgf * c))
    cdb = jnp.exp(one * (lgb * c))
    nrm = nrm_ref[...]
    tn = (((0,), (0,)), ((), ()))
    nt = (((1,), (1,)), ((), ()))

    def step(j, states):
        for u in range(RET_UNROLL):
            states = one_chunk(j * RET_UNROLL + u, states)
        return states

    def one_chunk(i, states):
        sf, sb = states
        rows = pl.ds(pl.multiple_of(i * c, c), c)
        q = q_ref[rows, :]
        k = k_ref[rows, :]
        v = v_ref[rows, :]
        scores = lax.dot_general(q, k, nt, preferred_element_type=F32) * decay
        part = jnp.dot(scores.astype(BF16), v, preferred_element_type=F32)
        part += jnp.dot((q.astype(F32) * qf_dec).astype(BF16), sf.astype(BF16), preferred_element_type=F32)
        part_ref[rows, :] = part
        sf = sf * cdf + lax.dot_general((k.astype(F32) * kf_dec).astype(BF16), v, tn, preferred_element_type=F32)

        rows = pl.ds(pl.multiple_of((nc - 1 - i) * c, c), c)
        q = q_ref[rows, :].astype(F32)
        k = k_ref[rows, :].astype(F32)
        cross_ref[rows, :] = jnp.dot((q * qb_dec).astype(BF16), sb.astype(BF16), preferred_element_type=F32)
        sb = sb * cdb + lax.dot_general((k * kb_dec).astype(BF16), v_ref[rows, :], tn, preferred_element_type=F32)
        return sf, sb

    zero = jnp.zeros((RET_DK, RET_DV), F32)
    lax.fori_loop(0, nc // RET_UNROLL, step, (zero, zero))

    def finish(i, carry):
        rows = pl.ds(pl.multiple_of(i * RET_FIN, RET_FIN), RET_FIN)
        out = part_ref[rows, :] + cross_ref[rows, :]
        o_ref[rows, :] = (rg_ref[rows, :].astype(F32) * (_rms_scale(out) * nrm)).astype(BF16)
        return carry

    lax.fori_loop(0, q_ref.shape[0] // RET_FIN, finish, 0)


def _retention(lgf, lgb, rq, rk, rv, rg, nrm, batch):
    n = rq.shape[0]
    bh = lambda b, h, *_: (b, h)
    return pl.pallas_call(
        _retention_kernel,
        grid_spec=pltpu.PrefetchScalarGridSpec(
            num_scalar_prefetch=2,
            grid=(batch, RET_HEADS),
            in_specs=[pl.BlockSpec((SEQ, RET_DK), bh), pl.BlockSpec((SEQ, RET_DK), bh),
                      pl.BlockSpec((SEQ, RET_DV), bh), pl.BlockSpec((SEQ, RET_DV), bh),
                      pl.BlockSpec((1, RET_DV), lambda b, h, *_: (0, h))],
            out_specs=pl.BlockSpec((SEQ, RET_DV), bh),
            scratch_shapes=[pltpu.VMEM((SEQ, RET_DV), F32), pltpu.VMEM((SEQ, RET_DV), F32)],
        ),
        out_shape=jax.ShapeDtypeStruct((n, RV), BF16),
        compiler_params=_params("parallel", "parallel"),
        name="retention",
    )(lgf, lgb, rq, rk, rv, rg, nrm)


def _attn_block(q, kw, vw, bias):
    lane = lax.broadcasted_iota(jnp.int32, q.shape, 1)
    head0_qk = (lane < 32) | ((lane >= 64) & (lane < 96))
    outs, lses = [], []
    for hh in range(2):
        qm = jnp.where(head0_qk if hh == 0 else jnp.logical_not(head0_qk), q, jnp.zeros_like(q))
        s = lax.dot_general(qm, kw, (((1,), (1,)), ((), ())), preferred_element_type=F32) + bias
        m = jnp.max(s, axis=-1, keepdims=True)
        p = jnp.exp(s - m)
        l = jnp.sum(p, axis=-1, keepdims=True)
        o = jnp.dot(p.astype(BF16), vw, preferred_element_type=F32)
        outs.append(o / l)
        lses.append(m + jnp.log(l))
    head0_v = lane < ATT_DH
    return jnp.where(head0_v, outs[0], outs[1]), jnp.where(head0_v, lses[0], lses[1])


def _attention_kernel(q0, k0, v0, q1, k1, v1, q2, k2, v2, o_ref, o1_ref, l1_ref, o2_ref, l2_ref, bias_ref):
    qb, kw = ATT_QB, ATT_KW
    ji = lax.broadcasted_iota(jnp.int32, (qb, kw), 1) - lax.broadcasted_iota(jnp.int32, (qb, kw), 0)
    for n in range(3):
        rel = ji - n * ATT_HALF_SPAN
        bias_ref[n] = jnp.where((rel >= -ATT_HALF_SPAN) & (rel <= ATT_HALF_SPAN), 0.0, MASK_NEG).astype(F32)

    def group(q_ref, k_ref, v_ref, dil, finish):
        sub_len = SEQ // dil
        nblk = sub_len // qb

        def body(jt, carry):
            for u in range(ATT_UNROLL):
                it = jt * ATT_UNROLL + u
                r = it // nblk
                l0 = (it % nblk) * qb
                kstart = pl.multiple_of(jnp.clip(l0 - ATT_HALF_SPAN, 0, sub_len - kw), ATT_HALF_SPAN)
                bias = bias_ref[(l0 - kstart) // ATT_HALF_SPAN]
                q = q_ref[0, 0, r, pl.ds(pl.multiple_of(l0, qb), qb), :]
                kwin = k_ref[0, 0, r, pl.ds(kstart, kw), :]
                vwin = v_ref[0, 0, r, pl.ds(kstart, kw), :]
                o, lse = _attn_block(q, kwin, vwin, bias)
                finish(l0, r, o, lse)
            return carry

        lax.fori_loop(0, dil * nblk // ATT_UNROLL, body, 0)

    def store_strided(o_nat, l_nat, dil):
        def finish(l0, r, o, lse):
            rows = pl.ds(l0 * dil + r, ATT_QB, stride=dil)
            o_nat[rows, :] = o
            l_nat[rows, :] = lse
        return finish

    group(q2, k2, v2, ATT_DILATIONS[2], store_strided(o2_ref, l2_ref, ATT_DILATIONS[2]))
    group(q1, k1, v1, ATT_DILATIONS[1], store_strided(o1_ref, l1_ref, ATT_DILATIONS[1]))

    def merge(l0, r, o, lse):
        rows = pl.ds(pl.multiple_of(l0, ATT_QB), ATT_QB)
        la, lb = l1_ref[rows, :], l2_ref[rows, :]
        mx = jnp.maximum(lse, jnp.maximum(la, lb))
        w0, w1, w2 = jnp.exp(lse - mx), jnp.exp(la - mx), jnp.exp(lb - mx)
        num = w0 * o + w1 * o1_ref[rows, :] + w2 * o2_ref[rows, :]
        o_ref[rows, :] = (num / (w0 + w1 + w2)).astype(BF16)

    group(q0, k0, v0, ATT_DILATIONS[0], merge)


def _attention(qkv, batch):
    in_specs = []
    for g, dil in enumerate(ATT_DILATIONS):
        for _ in range(3):
            in_specs.append(pl.BlockSpec((1, 1, dil, SEQ // dil, LANES), lambda b, p: (b, p, 0, 0, 0)))
    args = []
    for g in range(3):
        args += [qkv[0 * 3 + g], qkv[1 * 3 + g], qkv[2 * 3 + g]]
    return pl.pallas_call(
        _attention_kernel,
        grid=(batch, ATT_PAIRS),
        in_specs=in_specs,
        out_specs=pl.BlockSpec((SEQ, LANES), lambda b, p: (b, p)),
        out_shape=jax.ShapeDtypeStruct((batch * SEQ, A_OUT), BF16),
        scratch_shapes=[pltpu.VMEM((SEQ, LANES), F32) for _ in range(4)]
        + [pltpu.VMEM((3, ATT_QB, ATT_KW), F32)],
        compiler_params=_params("parallel", "parallel"),
        name="dilated_attention",
    )(*args)


def _mix_out_kernel(a_ref, att_ref, g_ref, x_ref, wr_ref, wa_ref, wo_ref, nf_ref, rhi_ref, rlo_ref,
                    x1_ref, h2_ref, aff_ref):
    bra = jnp.dot(a_ref[...], wr_ref[...], preferred_element_type=F32)
    brb = jnp.dot(att_ref[...], wa_ref[...], preferred_element_type=F32)
    g = g_ref[...].astype(F32)
    merged = g[:, :D_MODEL] * bra + g[:, D_MODEL:] * brb
    x1 = x_ref[...] + jnp.dot(merged.astype(BF16), wo_ref[...], preferred_element_type=F32)
    x1_ref[...] = x1
    h2 = _rms_scale(x1) * nf_ref[...]
    hi = h2.astype(BF16)
    lo = (h2 - hi.astype(F32)).astype(BF16)
    h2_ref[...] = hi
    nt = (((1,), (1,)), ((), ()))
    logits = (lax.dot_general(rhi_ref[...], hi, nt, preferred_element_type=F32)
              + lax.dot_general(rhi_ref[...], lo, nt, preferred_element_type=F32)
              + lax.dot_general(rlo_ref[...], hi, nt, preferred_element_type=F32))
    e = jnp.exp(logits - jnp.max(logits, axis=0, keepdims=True))
    aff_ref[...] = e / jnp.sum(e, axis=0, keepdims=True)


def _mix_out(a, att, g, x2d, wr, wa, wo, nf, rhi, rlo):
    n = x2d.shape[0]
    tm = PROJ_TM
    row = lambda i: (i, 0)
    fixed = lambda i: (0, 0)
    return pl.pallas_call(
        _mix_out_kernel,
        grid=(n // tm,),
        in_specs=[pl.BlockSpec((tm, RV), row), pl.BlockSpec((tm, A_OUT), row), pl.BlockSpec((tm, 2 * D_MODEL), row),
                  pl.BlockSpec((tm, D_MODEL), row), pl.BlockSpec(wr.shape, fixed), pl.BlockSpec(wa.shape, fixed),
                  pl.BlockSpec(wo.shape, fixed), pl.BlockSpec((1, D_MODEL), fixed),
                  pl.BlockSpec(rhi.shape, fixed), pl.BlockSpec(rlo.shape, fixed)],
        out_specs=[pl.BlockSpec((tm, D_MODEL), row), pl.BlockSpec((tm, D_MODEL), row),
                   pl.BlockSpec((N_EXPERTS, tm), lambda i: (0, i))],
        out_shape=[jax.ShapeDtypeStruct((n, D_MODEL), F32), jax.ShapeDtypeStruct((n, D_MODEL), BF16),
                   jax.ShapeDtypeStruct((N_EXPERTS, n), F32)],
        compiler_params=_params("parallel"),
        name="mix_out",
    )(a, att, g, x2d, wr, wa, wo, nf, rhi, rlo)


def _topk_kernel(aff_ref, spos_ref, offs_ref, rng_ref, *, cap):
    nchunk = aff_ref.shape[1]
    n_tiles = cap // FFN_TS
    li = lax.broadcasted_iota(jnp.int32, (LANES, LANES), 0)
    lj = lax.broadcasted_iota(jnp.int32, (LANES, LANES), 1)
    upper = jnp.where(li <= lj, 1.0, 0.0).astype(BF16)
    ci = lax.broadcasted_iota(jnp.int32, (nchunk, nchunk), 0)
    cj = lax.broadcasted_iota(jnp.int32, (nchunk, nchunk), 1)
    lower = jnp.where(cj < ci, 1.0, 0.0).astype(BF16)
    lane8 = lax.broadcasted_iota(jnp.int32, (8, LANES), 1)
    sub8 = lax.broadcasted_iota(jnp.int32, (8, LANES), 0)

    def total(x):
        return jnp.sum(jnp.sum(x, axis=0, keepdims=True), axis=1, keepdims=True)

    def cumsum(mask):
        within = jnp.dot(mask.astype(BF16), upper, preferred_element_type=F32)
        tot = jnp.broadcast_to(within[:, LANES - 1:LANES], within.shape).astype(BF16)
        before = jnp.dot(lower, tot, preferred_element_type=F32)
        return within + before, before

    def per_expert(e, carry):
        bits = pltpu.bitcast(aff_ref[e], jnp.int32)

        def search(i, thr):
            cand = thr | (1 << (30 - i))
            cnt = total(jnp.where(bits >= cand, 1, 0))
            return jnp.where(cnt >= cap, cand, thr)

        thr = lax.fori_loop(0, 31, search, jnp.zeros((1, 1), jnp.int32))
        gt = bits > thr
        eq = bits == thr
        need = (cap - total(jnp.where(gt, 1, 0))).astype(F32)
        eq_f = jnp.where(eq, 1.0, 0.0)
        eq_cum, _ = cumsum(eq_f)
        sel = gt | (eq & (eq_cum - eq_f < need))
        cum, before = cumsum(jnp.where(sel, 1.0, 0.0))
        spos_ref[e] = jnp.where(sel, cum - 1.0, -1.0).astype(jnp.int32)
        offs_ref[e] = before.astype(jnp.int32)
        first = jnp.zeros((8, LANES), jnp.int32)
        for s in range(n_tiles):
            a = total(jnp.where(cum <= float(s * FFN_TS), 1, 0))
            b = total(jnp.where(cum < float((s + 1) * FFN_TS), 1, 0))
            first = jnp.where((lane8 == s) & (sub8 == 0), a, first)
            first = jnp.where((lane8 == s) & (sub8 == 1), b, first)
        rng_ref[e] = first
        return carry

    lax.fori_loop(0, N_EXPERTS, per_expert, 0)


def _topk(aff3, cap):
    e, nchunk, _ = aff3.shape
    return pl.pallas_call(
        functools.partial(_topk_kernel, cap=cap),
        out_shape=[jax.ShapeDtypeStruct((e, nchunk, LANES), jnp.int32),
                   jax.ShapeDtypeStruct((e, nchunk, LANES), jnp.int32),
                   jax.ShapeDtypeStruct((e, 8, LANES), jnp.int32)],
        compiler_params=pltpu.CompilerParams(vmem_limit_bytes=VMEM_LIMIT),
        name="expert_choice_topk",
    )(aff3)


def _ffn_kernel(rng_ref, spos_ref, aff_ref, h2_hbm, wg_ref, wu_ref, wd_ref, out_ref, buf_ref, sem_ref, xs_ref,
                gate_ref):
    e = pl.program_id(0)
    s = pl.program_id(1)
    n_tiles = pl.num_programs(1)
    step = e * n_tiles + s
    ts, ch, ring = FFN_TS, FFN_CH, FFN_RING
    c0 = rng_ref[step * 2]
    n = rng_ref[step * 2 + 1] - c0 + 1

    def chunk_copy(c, slot):
        return pltpu.make_async_copy(h2_hbm.at[pl.ds(pl.multiple_of(c * ch, ch), ch), :], buf_ref.at[slot],
                                     sem_ref.at[slot])

    def start_head(first, count):
        for k in range(ring):
            @pl.when(k < count)
            def _(k=k):
                chunk_copy(first + k, k).start()

    @pl.when(step == 0)
    def _():
        start_head(c0, n)

    xs_ref[...] = jnp.zeros_like(xs_ref)
    gate_ref[...] = jnp.zeros_like(gate_ref)
    slot_ids = s * ts + lax.broadcasted_iota(jnp.int32, (ts, ch), 0)

    def body(k, carry):
        slot = k % ring
        c = c0 + k
        chunk_copy(c, slot).wait()
        hit = slot_ids == spos_ref[0, pl.ds(c, 1), :]
        xs_ref[...] += jnp.dot(jnp.where(hit, 1.0, 0.0).astype(BF16), buf_ref[slot], preferred_element_type=F32)
        gate_ref[...] += jnp.sum(jnp.where(hit, aff_ref[0, pl.ds(c, 1), :], 0.0), axis=1, keepdims=True)

        @pl.when(k + ring < n)
        def _():
            chunk_copy(c + ring, slot).start()

        return carry

    lax.fori_loop(0, n, body, 0)

    @pl.when(step + 1 < pl.num_programs(0) * n_tiles)
    def _():
        nc0 = rng_ref[step * 2 + 2]
        start_head(nc0, rng_ref[step * 2 + 3] - nc0 + 1)

    xs = xs_ref[...].astype(BF16)
    gate = jnp.dot(xs, wg_ref[0], preferred_element_type=F32)
    up = jnp.dot(xs, wu_ref[0], preferred_element_type=F32)
    hid = (gate * _sigmoid(gate) * up).astype(BF16)
    out_ref[0] = (jnp.dot(hid, wd_ref[0], preferred_element_type=F32) * gate_ref[...]).astype(BF16)


def _ffn(rng, spos_ch, aff_ch, h2, wg, wu, wd, cap):
    n_tiles = cap // FFN_TS
    nch = spos_ch.shape[1]
    wmap = lambda e, s, *_: (e, 0, 0)
    return pl.pallas_call(
        _ffn_kernel,
        grid_spec=pltpu.PrefetchScalarGridSpec(
            num_scalar_prefetch=1,
            grid=(N_EXPERTS, n_tiles),
            in_specs=[pl.BlockSpec((1, nch, FFN_CH), wmap),
                      pl.BlockSpec((1, nch, FFN_CH), wmap),
                      pl.BlockSpec(memory_space=pl.ANY),
                      pl.BlockSpec((1, D_MODEL, EXPERT_HIDDEN), wmap),
                      pl.BlockSpec((1, D_MODEL, EXPERT_HIDDEN), wmap),
                      pl.BlockSpec((1, EXPERT_HIDDEN, D_MODEL), wmap)],
            out_specs=pl.BlockSpec((1, FFN_TS, D_MODEL), lambda e, s, *_: (e, s, 0)),
            scratch_shapes=[pltpu.VMEM((FFN_RING, FFN_CH, D_MODEL), BF16), pltpu.SemaphoreType.DMA((FFN_RING,)),
                            pltpu.VMEM((FFN_TS, D_MODEL), F32), pltpu.VMEM((FFN_TS, 1), F32)],
        ),
        out_shape=jax.ShapeDtypeStruct((N_EXPERTS, cap, D_MODEL), BF16),
        compiler_params=_params("arbitrary", "arbitrary"),
        name="expert_ffn",
    )(rng, spos_ch, aff_ch, h2, wg, wu, wd)


def _combine_kernel(base_ref, spos_ref, x1_ref, nf_ref, eo_hbm, o_ref, stage_ref, sem_ref, tail_ref, tail_sem,
                    acc_ref, *, cap):
    i = pl.program_id(0)
    tm, win = CMB_TM, CMB_WIN
    lane_ids = lax.broadcasted_iota(jnp.int32, (tm, win), 1)

    def window_start(tile, e):
        return jnp.minimum((base_ref[tile * N_EXPERTS + e] // CMB_ALIGN) * CMB_ALIGN, cap - win)

    def window_copy(tile, e, buf):
        a0 = pl.multiple_of(window_start(tile, e), CMB_ALIGN)
        return pltpu.make_async_copy(eo_hbm.at[e, pl.ds(a0, win), :], stage_ref.at[buf, pl.ds(e * win, win), :],
                                     sem_ref.at[buf, e])

    @pl.when(i == 0)
    def _():
        for e in range(N_EXPERTS):
            window_copy(0, e, 0).start()

    @pl.when(i + 1 < pl.num_programs(0))
    def _():
        for e in range(N_EXPERTS):
            window_copy(i + 1, e, (i + 1) % 2).start()

    buf = i % 2
    pieces = []
    for e in range(N_EXPERTS):
        window_copy(i, e, buf).wait()
        pieces.append(jnp.where(spos_ref[:, e:e + 1] == window_start(i, e) + lane_ids, 1.0, 0.0).astype(BF16))
    onehot = jnp.concatenate(pieces, axis=1)
    acc_ref[...] = x1_ref[...] + jnp.dot(onehot, stage_ref[buf], preferred_element_type=F32)

    for e in range(N_EXPERTS):
        end = base_ref[(i + 1) * N_EXPERTS + e]
        covered = window_start(i, e) + win

        @pl.when(end > covered)
        def _(e=e, end=end, covered=covered):
            def extra(w, carry):
                lo = covered + w * win
                a0 = pl.multiple_of(jnp.minimum(lo, cap - win), CMB_ALIGN)
                cp = pltpu.make_async_copy(eo_hbm.at[e, pl.ds(a0, win), :], tail_ref, tail_sem)
                cp.start()
                cp.wait()
                col = spos_ref[:, e:e + 1]
                hit = (col == a0 + lane_ids) & (col >= lo)
                acc_ref[...] += jnp.dot(jnp.where(hit, 1.0, 0.0).astype(BF16), tail_ref[...],
                                        preferred_element_type=F32)
                return carry

            lax.fori_loop(0, (end - covered + win - 1) // win, extra, 0)

    o_ref[...] = _rms_scale(acc_ref[...]) * nf_ref[...]


def _combine(base, spos_t, x1, nf, eo):
    n = x1.shape[0]
    cap = eo.shape[1]
    tm = CMB_TM
    row = lambda i, *_: (i, 0)
    return pl.pallas_call(
        functools.partial(_combine_kernel, cap=cap),
        grid_spec=pltpu.PrefetchScalarGridSpec(
            num_scalar_prefetch=1,
            grid=(n // tm,),
            in_specs=[pl.BlockSpec((tm, N_EXPERTS), row),
                      pl.BlockSpec((tm, D_MODEL), row), pl.BlockSpec((1, D_MODEL), lambda i, *_: (0, 0)),
                      pl.BlockSpec(memory_space=pl.ANY)],
            out_specs=pl.BlockSpec((tm, D_MODEL), row),
            scratch_shapes=[pltpu.VMEM((2, N_EXPERTS * CMB_WIN, D_MODEL), BF16),
                            pltpu.SemaphoreType.DMA((2, N_EXPERTS)),
                            pltpu.VMEM((CMB_WIN, D_MODEL), BF16), pltpu.SemaphoreType.DMA,
                            pltpu.VMEM((tm, D_MODEL), F32)],
        ),
        out_shape=jax.ShapeDtypeStruct((n, D_MODEL), F32),
        compiler_params=_params("arbitrary"),
        name="combine_final",
    )(base, spos_t, x1, nf, eo)


def _moe(x1, h2, aff_t, p):
    n = x1.shape[0]
    cap = EC_CAPACITY_FACTOR * n // N_EXPERTS
    spos, offs, rng = _topk(aff_t.reshape(N_EXPERTS, n // LANES, LANES), cap)
    n_tiles = cap // FFN_TS
    rng = (rng[:, :2, :n_tiles] // FFN_CH).transpose(0, 2, 1).reshape(-1)
    eo = _ffn(rng, spos.reshape(N_EXPERTS, n // FFN_CH, FFN_CH), aff_t.reshape(N_EXPERTS, n // FFN_CH, FFN_CH), h2,
              p["w_exp_gate"], p["w_exp_up"], p["w_exp_down"], cap)
    base = offs[:, ::CMB_TM // LANES, 0].T
    base = jnp.concatenate([base, jnp.full((1, N_EXPERTS), cap, jnp.int32)], axis=0).reshape(-1)
    return _combine(base, spos.reshape(N_EXPERTS, n).T, x1, p["norm_final"], eo)


def _pair_perm():
    base = np.concatenate([np.arange(0, 32), np.arange(64, 96), np.arange(32, 64), np.arange(96, 128)])
    return np.concatenate([base + LANES * b for b in range(AQ // LANES)])


def _rope_tables():
    pos = jnp.arange(SEQ, dtype=F32)
    ret_inv = 1.0 / (ROPE_THETA ** jnp.linspace(0.0, 1.0, RET_DK // 2, dtype=F32))
    ang = pos[:, None] * ret_inv[None, :]
    ret_cos = jnp.concatenate([jnp.cos(ang), jnp.cos(ang)], axis=-1)
    ret_sin = jnp.concatenate([-jnp.sin(ang), jnp.sin(ang)], axis=-1)
    att_inv = ROPE_THETA ** (-jnp.arange(0, ATT_DH, 2, dtype=F32) / ATT_DH)
    ang = pos[:, None] * att_inv[None, :]
    c, s = jnp.cos(ang), jnp.sin(ang)
    att_cos = jnp.concatenate([c, c, c, c], axis=-1)
    att_sin = jnp.concatenate([-s, -s, s, s], axis=-1)
    att_tabs = []
    for dil in ATT_DILATIONS:
        for tab in (att_cos, att_sin):
            att_tabs.append(tab.reshape(SEQ // dil, dil, LANES).transpose(1, 0, 2))
    return ret_cos, ret_sin, att_tabs


def _encoder(x, p):
    batch = x.shape[0]
    x2d = x.reshape(batch * SEQ, D_MODEL)
    rq, rk, rv, rg, gates = _proj_ret(x2d, p["norm_mix"], p["w_ret"], p["ret_cos"], p["ret_sin"])
    qkv = _proj_att(x2d, p["norm_mix"], p["w_att"], p["att_tabs"], batch)
    a = _retention(p["lgf"], p["lgb"], rq, rk, rv, rg, p["ret_norm"], batch)
    att = _attention(qkv, batch)
    x1, h2, aff_t = _mix_out(a, att, gates, x2d, p["w_ret_out"], p["w_att_out"], p["w_out"], p["norm_ffn"],
                             p["router_hi"], p["router_lo"])
    return _moe(x1, h2, aff_t, p).reshape(batch, SEQ, D_MODEL)


def kernel(x_prompt, x_sample, norm_mix, w_in, ret_log_decay_fwd, ret_log_decay_bwd, ret_norm, w_ret_out,
           w_att_out, w_out, norm_ffn, w_router, w_exp_gate, w_exp_up, w_exp_down, norm_final):
    w = w_in[0]
    c_rv = 2 * RQ
    c_rg = c_rv + RV
    c_aq = c_rg + RV
    c_ak = c_aq + AQ
    c_av = c_ak + AQ
    c_gate = c_av + AQ
    perm = _pair_perm()
    w_aq = w[:, c_aq:c_ak][:, perm] * (ATT_DH ** -0.5)
    w_ak = w[:, c_ak:c_av][:, perm]
    w_av = w[:, c_av:c_gate]
    ret_cos, ret_sin, att_tabs = _rope_tables()
    router_t = w_router[0].T
    router_hi = router_t.astype(BF16)
    p = {
        "norm_mix": norm_mix,
        "w_ret": jnp.concatenate([w[:, :c_aq], w[:, c_gate:]], axis=1).astype(BF16),
        "w_att": jnp.concatenate([w_aq, w_ak, w_av], axis=1).astype(BF16),
        "ret_cos": ret_cos, "ret_sin": ret_sin, "att_tabs": att_tabs,
        "lgf": ret_log_decay_fwd[0], "lgb": ret_log_decay_bwd[0],
        "ret_norm": ret_norm,
        "w_ret_out": w_ret_out[0].astype(BF16), "w_att_out": w_att_out[0].astype(BF16),
        "w_out": w_out[0].astype(BF16),
        "norm_ffn": norm_ffn,
        "router_hi": router_hi, "router_lo": (router_t - router_hi.astype(F32)).astype(BF16),
        "w_exp_gate": w_exp_gate[0].astype(BF16), "w_exp_up": w_exp_up[0].astype(BF16),
        "w_exp_down": w_exp_down[0].astype(BF16),
        "norm_final": norm_final.reshape(1, D_MODEL),
    }
    return _encoder(x_prompt, p), _encoder(x_sample, p)
```

```python
import functools

import numpy as np
import jax
import jax.numpy as jnp
from jax import lax
from jax.experimental import pallas as pl
from jax.experimental.pallas import tpu as pltpu

F32 = jnp.float32
BF16 = jnp.bfloat16

D_MODEL = 1024
SEQ = 4096
RET_HEADS = 4
RET_DK = 128
RET_DV = 256
RET_CHUNK = 128
RET_UNROLL = 4
RET_FIN = 512
ATT_DILATIONS = (1, 4, 16)
ATT_HALF_SPAN = 64
ATT_HEADS_PER_GROUP = 8
ATT_DH = 64
ATT_PAIRS = ATT_HEADS_PER_GROUP // 2
N_EXPERTS = 16
EC_CAPACITY_FACTOR = 2
EXPERT_HIDDEN = 2048
ROPE_THETA = 10000.0
EPS = 1e-6
RQ = RET_HEADS * RET_DK
RV = RET_HEADS * RET_DV
AQ = 3 * ATT_HEADS_PER_GROUP * ATT_DH
A_OUT = ATT_HEADS_PER_GROUP * ATT_DH
LANES = 128

PROJ_TM = 512
ATT_QB = 128
ATT_KW = ATT_QB + 2 * ATT_HALF_SPAN
ATT_UNROLL = 8
FFN_TS = 256
TOK_ROW = D_MODEL + LANES
TOPK_SL = 512
CMB_TM = 256
CMB_WIN = 64
CMB_ALIGN = 16
MASK_NEG = -1e30
VMEM_LIMIT = 56 * 1024 * 1024


def _params(*sem):
    return pltpu.CompilerParams(dimension_semantics=sem, vmem_limit_bytes=VMEM_LIMIT)


def _sigmoid(x):
    return 1.0 / (1.0 + jnp.exp(-x))


def _rms_scale(x):
    return x * lax.rsqrt(jnp.mean(x * x, axis=-1, keepdims=True) + EPS)


def _rope(a, cos, sin):
    return a * cos + pltpu.roll(a, 64, 1) * sin


def _proj_ret_kernel(x_ref, nm_ref, w_ref, cos_ref, sin_ref, rq_ref, rk_ref, rv_ref, rg_ref, g_ref):
    h = (_rms_scale(x_ref[...]) * nm_ref[...]).astype(BF16)
    cos = cos_ref[...]
    sin = sin_ref[...]

    def proj(c0, width):
        return jnp.dot(h, w_ref[:, c0:c0 + width], preferred_element_type=F32)

    acc = proj(0, RQ)
    for hh in range(RET_HEADS):
        sl = slice(hh * LANES, (hh + 1) * LANES)
        rq_ref[:, sl] = _rope(acc[:, sl], cos, sin).astype(BF16)
    acc = proj(RQ, RQ)
    for hh in range(RET_HEADS):
        sl = slice(hh * LANES, (hh + 1) * LANES)
        rk_ref[:, sl] = (_rope(acc[:, sl], cos, sin) * (RET_DK ** -0.5)).astype(BF16)
    for c in range(RV // 512):
        rv_ref[:, c * 512:(c + 1) * 512] = proj(2 * RQ + c * 512, 512).astype(BF16)
    for c in range(RV // 512):
        acc = proj(2 * RQ + RV + c * 512, 512)
        rg_ref[:, c * 512:(c + 1) * 512] = (acc * _sigmoid(acc)).astype(BF16)
    for c in range(2 * D_MODEL // 512):
        acc = proj(2 * RQ + 2 * RV + c * 512, 512)
        g_ref[:, c * 512:(c + 1) * 512] = _sigmoid(acc).astype(BF16)


def _proj_ret(x2d, nm, w, cos, sin):
    n = x2d.shape[0]
    nt = SEQ // PROJ_TM
    tm = PROJ_TM
    row = lambda i: (i, 0)
    fixed = lambda i: (0, 0)
    tab = lambda i: (i % nt, 0)
    return pl.pallas_call(
        _proj_ret_kernel,
        grid=(n // tm,),
        in_specs=[pl.BlockSpec((tm, D_MODEL), row), pl.BlockSpec((1, D_MODEL), fixed),
                  pl.BlockSpec(w.shape, fixed), pl.BlockSpec((tm, LANES), tab), pl.BlockSpec((tm, LANES), tab)],
        out_specs=[pl.BlockSpec((tm, RQ), row), pl.BlockSpec((tm, RQ), row), pl.BlockSpec((tm, RV), row),
                   pl.BlockSpec((tm, RV), row), pl.BlockSpec((tm, 2 * D_MODEL), row)],
        out_shape=[jax.ShapeDtypeStruct((n, RQ), BF16), jax.ShapeDtypeStruct((n, RQ), BF16),
                   jax.ShapeDtypeStruct((n, RV), BF16), jax.ShapeDtypeStruct((n, RV), BF16),
                   jax.ShapeDtypeStruct((n, 2 * D_MODEL), BF16)],
        compiler_params=_params("parallel"),
        name="proj_ret",
    )(x2d, nm, w, cos, sin)


def _proj_att_kernel(x_ref, nm_ref, w_ref, c0_ref, s0_ref, c1_ref, s1_ref, c2_ref, s2_ref, *rest):
    out_refs = rest[:9]
    scr_ref = rest[9]
    h = (_rms_scale(x_ref[...]) * nm_ref[...]).astype(BF16)
    tabs = ((c0_ref, s0_ref), (c1_ref, s1_ref), (c2_ref, s2_ref))
    tm = x_ref.shape[0]
    for sec in range(3):
        for g, dil in enumerate(ATT_DILATIONS):
            c0 = (sec * 3 + g) * A_OUT
            acc = jnp.dot(h, w_ref[:, c0:c0 + A_OUT], preferred_element_type=F32)
            out_ref = out_refs[sec * 3 + g]
            if dil > 1:
                for p in range(ATT_PAIRS):
                    scr_ref[p] = acc[:, p * LANES:(p + 1) * LANES]
            for r in range(dil):
                for p in range(ATT_PAIRS):
                    if dil == 1:
                        a = acc[:, p * LANES:(p + 1) * LANES]
                    else:
                        a = scr_ref[p, pl.ds(r, tm // dil, stride=dil), :]
                    if sec < 2:
                        a = _rope(a, tabs[g][0][r], tabs[g][1][r])
                    out_ref[0, p, r] = a.astype(BF16)


def _proj_att(x2d, nm, w, tabs, batch):
    n = x2d.shape[0]
    tm = PROJ_TM
    nt = SEQ // tm
    row = lambda i: (i, 0)
    fixed = lambda i: (0, 0)
    in_specs = [pl.BlockSpec((tm, D_MODEL), row), pl.BlockSpec((1, D_MODEL), fixed), pl.BlockSpec(w.shape, fixed)]
    for dil in ATT_DILATIONS:
        for _ in range(2):
            in_specs.append(pl.BlockSpec((dil, tm // dil, LANES), lambda i: (0, i % nt, 0)))
    out_specs, out_shape = [], []
    for _ in range(3):
        for dil in ATT_DILATIONS:
            out_specs.append(pl.BlockSpec((1, ATT_PAIRS, dil, tm // dil, LANES),
                                          lambda i: (i // nt, 0, 0, i % nt, 0)))
            out_shape.append(jax.ShapeDtypeStruct((batch, ATT_PAIRS, dil, SEQ // dil, LANES), BF16))
    return pl.pallas_call(
        _proj_att_kernel,
        grid=(n // tm,),
        in_specs=in_specs,
        out_specs=out_specs,
        out_shape=out_shape,
        scratch_shapes=[pltpu.VMEM((ATT_PAIRS, tm, LANES), F32)],
        compiler_params=_params("parallel"),
        name="proj_att",
    )(x2d, nm, w, *tabs)


def _retention_kernel(lgf_ref, lgb_ref, q_ref, k_ref, v_ref, rg_ref, nrm_ref, o_ref, part_ref, cross_ref):
    hd = pl.program_id(1)
    lgf = lgf_ref[hd]
    lgb = lgb_ref[hd]
    c = RET_CHUNK
    nc = q_ref.shape[0] // c
    ti = lax.broadcasted_iota(jnp.int32, (c, c), 0)
    si = lax.broadcasted_iota(jnp.int32, (c, c), 1)
    diff = (ti - si).astype(F32)
    fwd = diff >= 0.0
    decay = jnp.where(fwd, jnp.exp(lgf * jnp.where(fwd, diff, 0.0)), jnp.exp(lgb * jnp.where(fwd, 0.0, -diff)))
    idx = lax.broadcasted_iota(jnp.int32, (c, 1), 0).astype(F32)
    qf_dec = jnp.exp(lgf * (idx + 1.0))
    kf_dec = jnp.exp(lgf * (c - 1.0 - idx))
    qb_dec = jnp.exp(lgb * (c - idx))
    kb_dec = jnp.exp(lgb * idx)
    one = jnp.ones((1, 1), F32)
    cdf = jnp.exp(one * (lgf * c))
    cdb = jnp.exp(one * (lgb * c))
    nrm = nrm_ref[...]
    tn = (((0,), (0,)), ((), ()))
    nt = (((1,), (1,)), ((), ()))

    def step(j, states):
        for u in range(RET_UNROLL):
            states = one_chunk(j * RET_UNROLL + u, states)
        return states

    def one_chunk(i, states):
        sf, sb = states
        rows = pl.ds(pl.multiple_of(i * c, c), c)
        q = q_ref[rows, :]
        k = k_ref[rows, :]
        v = v_ref[rows, :]
        scores = lax.dot_general(q, k, nt, preferred_element_type=F32) * decay
        part = jnp.dot(scores.astype(BF16), v, preferred_element_type=F32)
        part += jnp.dot((q.astype(F32) * qf_dec).astype(BF16), sf.astype(BF16), preferred_element_type=F32)
        part_ref[rows, :] = part
        sf = sf * cdf + lax.dot_general((k.astype(F32) * kf_dec).astype(BF16), v, tn, preferred_element_type=F32)

        rows = pl.ds(pl.multiple_of((nc - 1 - i) * c, c), c)
        q = q_ref[rows, :].astype(F32)
        k = k_ref[rows, :].astype(F32)
        cross_ref[rows, :] = jnp.dot((q * qb_dec).astype(BF16), sb.astype(BF16), preferred_element_type=F32)
        sb = sb * cdb + lax.dot_general((k * kb_dec).astype(BF16), v_ref[rows, :], tn, preferred_element_type=F32)
        return sf, sb

    zero = jnp.zeros((RET_DK, RET_DV), F32)
    lax.fori_loop(0, nc // RET_UNROLL, step, (zero, zero))

    def finish(i, carry):
        rows = pl.ds(pl.multiple_of(i * RET_FIN, RET_FIN), RET_FIN)
        out = part_ref[rows, :] + cross_ref[rows, :]
        o_ref[rows, :] = (rg_ref[rows, :].astype(F32) * (_rms_scale(out) * nrm)).astype(BF16)
        return carry

    lax.fori_loop(0, q_ref.shape[0] // RET_FIN, finish, 0)


def _retention(lgf, lgb, rq, rk, rv, rg, nrm, batch):
    n = rq.shape[0]
    bh = lambda b, h, *_: (b, h)
    return pl.pallas_call(
        _retention_kernel,
        grid_spec=pltpu.PrefetchScalarGridSpec(
            num_scalar_prefetch=2,
            grid=(batch, RET_HEADS),
            in_specs=[pl.BlockSpec((SEQ, RET_DK), bh), pl.BlockSpec((SEQ, RET_DK), bh),
                      pl.BlockSpec((SEQ, RET_DV), bh), pl.BlockSpec((SEQ, RET_DV), bh),
                      pl.BlockSpec((1, RET_DV), lambda b, h, *_: (0, h))],
            out_specs=pl.BlockSpec((SEQ, RET_DV), bh),
            scratch_shapes=[pltpu.VMEM((SEQ, RET_DV), F32), pltpu.VMEM((SEQ, RET_DV), F32)],
        ),
        out_shape=jax.ShapeDtypeStruct((n, RV), BF16),
        compiler_params=_params("parallel", "parallel"),
        name="retention",
    )(lgf, lgb, rq, rk, rv, rg, nrm)


def _attn_block(q, kw, vw, bias):
    lane = lax.broadcasted_iota(jnp.int32, q.shape, 1)
    head0_qk = (lane < 32) | ((lane >= 64) & (lane < 96))
    outs, lses = [], []
    for hh in range(2):
        qm = jnp.where(head0_qk if hh == 0 else jnp.logical_not(head0_qk), q, jnp.zeros_like(q))
        s = lax.dot_general(qm, kw, (((1,), (1,)), ((), ())), preferred_element_type=F32) + bias
        m = jnp.max(s, axis=-1, keepdims=True)
        p = jnp.exp(s - m)
        l = jnp.sum(p, axis=-1, keepdims=True)
        o = jnp.dot(p.astype(BF16), vw, preferred_element_type=F32)
        outs.append(o / l)
        lses.append(m + jnp.log(l))
    head0_v = lane < ATT_DH
    return jnp.where(head0_v, outs[0], outs[1]), jnp.where(head0_v, lses[0], lses[1])


def _attention_kernel(q0, k0, v0, q1, k1, v1, q2, k2, v2, o_ref, o1_ref, l1_ref, o2_ref, l2_ref, bias_ref):
    qb, kw = ATT_QB, ATT_KW
    ji = lax.broadcasted_iota(jnp.int32, (qb, kw), 1) - lax.broadcasted_iota(jnp.int32, (qb, kw), 0)
    for n in range(3):
        rel = ji - n * ATT_HALF_SPAN
        bias_ref[n] = jnp.where((rel >= -ATT_HALF_SPAN) & (rel <= ATT_HALF_SPAN), 0.0, MASK_NEG).astype(F32)

    def group(q_ref, k_ref, v_ref, dil, finish):
        sub_len = SEQ // dil
        nblk = sub_len // qb

        def body(jt, carry):
            for u in range(ATT_UNROLL):
                it = jt * ATT_UNROLL + u
                r = it // nblk
                l0 = (it % nblk) * qb
                kstart = pl.multiple_of(jnp.clip(l0 - ATT_HALF_SPAN, 0, sub_len - kw), ATT_HALF_SPAN)
                bias = bias_ref[(l0 - kstart) // ATT_HALF_SPAN]
                q = q_ref[0, 0, r, pl.ds(pl.multiple_of(l0, qb), qb), :]
                kwin = k_ref[0, 0, r, pl.ds(kstart, kw), :]
                vwin = v_ref[0, 0, r, pl.ds(kstart, kw), :]
                o, lse = _attn_block(q, kwin, vwin, bias)
                finish(l0, r, o, lse)
            return carry

        lax.fori_loop(0, dil * nblk // ATT_UNROLL, body, 0)

    def store_strided(o_nat, l_nat, dil):
        def finish(l0, r, o, lse):
            rows = pl.ds(l0 * dil + r, ATT_QB, stride=dil)
            o_nat[rows, :] = o
            l_nat[rows, :] = lse
        return finish

    group(q2, k2, v2, ATT_DILATIONS[2], store_strided(o2_ref, l2_ref, ATT_DILATIONS[2]))
    group(q1, k1, v1, ATT_DILATIONS[1], store_strided(o1_ref, l1_ref, ATT_DILATIONS[1]))

    def merge(l0, r, o, lse):
        rows = pl.ds(pl.multiple_of(l0, ATT_QB), ATT_QB)
        la, lb = l1_ref[rows, :], l2_ref[rows, :]
        mx = jnp.maximum(lse, jnp.maximum(la, lb))
        w0, w1, w2 = jnp.exp(lse - mx), jnp.exp(la - mx), jnp.exp(lb - mx)
        num = w0 * o + w1 * o1_ref[rows, :] + w2 * o2_ref[rows, :]
        o_ref[rows, :] = (num / (w0 + w1 + w2)).astype(BF16)

    group(q0, k0, v0, ATT_DILATIONS[0], merge)


def _attention(qkv, batch):
    in_specs = []
    for g, dil in enumerate(ATT_DILATIONS):
        for _ in range(3):
            in_specs.append(pl.BlockSpec((1, 1, dil, SEQ // dil, LANES), lambda b, p: (b, p, 0, 0, 0)))
    args = []
    for g in range(3):
        args += [qkv[0 * 3 + g], qkv[1 * 3 + g], qkv[2 * 3 + g]]
    return pl.pallas_call(
        _attention_kernel,
        grid=(batch, ATT_PAIRS),
        in_specs=in_specs,
        out_specs=pl.BlockSpec((SEQ, LANES), lambda b, p: (b, p)),
        out_shape=jax.ShapeDtypeStruct((batch * SEQ, A_OUT), BF16),
        scratch_shapes=[pltpu.VMEM((SEQ, LANES), F32) for _ in range(4)]
        + [pltpu.VMEM((3, ATT_QB, ATT_KW), F32)],
        compiler_params=_params("parallel", "parallel"),
        name="dilated_attention",
    )(*args)


def _mix_out_kernel(a_ref, att_ref, g_ref, x_ref, wr_ref, wa_ref, wo_ref, nf_ref, rhi_ref, rlo_ref,
                    x1_ref, h2_ref, aff_ref):
    bra = jnp.dot(a_ref[...], wr_ref[...], preferred_element_type=F32)
    brb = jnp.dot(att_ref[...], wa_ref[...], preferred_element_type=F32)
    g = g_ref[...].astype(F32)
    merged = g[:, :D_MODEL] * bra + g[:, D_MODEL:] * brb
    x1 = x_ref[...] + jnp.dot(merged.astype(BF16), wo_ref[...], preferred_element_type=F32)
    x1_ref[...] = x1
    h2 = _rms_scale(x1) * nf_ref[...]
    hi = h2.astype(BF16)
    lo = (h2 - hi.astype(F32)).astype(BF16)
    nt = (((1,), (1,)), ((), ()))
    logits = (lax.dot_general(rhi_ref[...], hi, nt, preferred_element_type=F32)
              + lax.dot_general(rhi_ref[...], lo, nt, preferred_element_type=F32)
              + lax.dot_general(rlo_ref[...], hi, nt, preferred_element_type=F32))
    e = jnp.exp(logits - jnp.max(logits, axis=0, keepdims=True))
    aff = e / jnp.sum(e, axis=0, keepdims=True)
    aff_ref[...] = aff
    h2_ref[:, :D_MODEL] = h2
    pad = jnp.zeros((LANES - N_EXPERTS, aff.shape[1]), F32)
    h2_ref[:, D_MODEL:] = jnp.transpose(jnp.concatenate([aff, pad], axis=0))


def _mix_out(a, att, g, x2d, wr, wa, wo, nf, rhi, rlo):
    n = x2d.shape[0]
    tm = PROJ_TM
    row = lambda i: (i, 0)
    fixed = lambda i: (0, 0)
    return pl.pallas_call(
        _mix_out_kernel,
        grid=(n // tm,),
        in_specs=[pl.BlockSpec((tm, RV), row), pl.BlockSpec((tm, A_OUT), row), pl.BlockSpec((tm, 2 * D_MODEL), row),
                  pl.BlockSpec((tm, D_MODEL), row), pl.BlockSpec(wr.shape, fixed), pl.BlockSpec(wa.shape, fixed),
                  pl.BlockSpec(wo.shape, fixed), pl.BlockSpec((1, D_MODEL), fixed),
                  pl.BlockSpec(rhi.shape, fixed), pl.BlockSpec(rlo.shape, fixed)],
        out_specs=[pl.BlockSpec((tm, D_MODEL), row), pl.BlockSpec((tm, TOK_ROW), row),
                   pl.BlockSpec((N_EXPERTS, tm), lambda i: (0, i))],
        out_shape=[jax.ShapeDtypeStruct((n, D_MODEL), F32), jax.ShapeDtypeStruct((n, TOK_ROW), F32),
                   jax.ShapeDtypeStruct((N_EXPERTS, n), F32)],
        compiler_params=_params("parallel"),
        name="mix_out",
    )(a, att, g, x2d, wr, wa, wo, nf, rhi, rlo)


def _topk_kernel(aff_ref, spos_ref, offs_ref, idx_ref, *, cap):
    nchunk = aff_ref.shape[1]
    sl = TOPK_SL
    li = lax.broadcasted_iota(jnp.int32, (LANES, LANES), 0)
    lj = lax.broadcasted_iota(jnp.int32, (LANES, LANES), 1)
    upper = jnp.where(li <= lj, 1.0, 0.0).astype(BF16)
    ci = lax.broadcasted_iota(jnp.int32, (nchunk, nchunk), 0)
    cj = lax.broadcasted_iota(jnp.int32, (nchunk, nchunk), 1)
    lower = jnp.where(cj < ci, 1.0, 0.0).astype(BF16)
    row_ids = lax.broadcasted_iota(jnp.int32, (sl, nchunk), 1).astype(F32)
    slot_iota = lax.broadcasted_iota(jnp.int32, (sl, 1), 0).astype(F32)

    def total(x):
        return jnp.sum(jnp.sum(x, axis=0, keepdims=True), axis=1, keepdims=True)

    def cumsum(mask):
        within = jnp.dot(mask.astype(BF16), upper, preferred_element_type=F32)
        tot = jnp.broadcast_to(within[:, LANES - 1:LANES], within.shape).astype(BF16)
        before = jnp.dot(lower, tot, preferred_element_type=F32)
        return within + before, before

    def per_expert(e, carry):
        bits = pltpu.bitcast(aff_ref[e], jnp.int32)

        def search(i, thr):
            cand = thr | (1 << (30 - i))
            cnt = total(jnp.where(bits >= cand, 1, 0))
            return jnp.where(cnt >= cap, cand, thr)

        thr = lax.fori_loop(0, 31, search, jnp.zeros((1, 1), jnp.int32))
        gt = bits > thr
        eq = bits == thr
        need = (cap - total(jnp.where(gt, 1, 0))).astype(F32)
        eq_f = jnp.where(eq, 1.0, 0.0)
        eq_cum, _ = cumsum(eq_f)
        sel = gt | (eq & (eq_cum - eq_f < need))
        cum, before = cumsum(jnp.where(sel, 1.0, 0.0))
        spos_ref[e] = jnp.where(sel, cum - 1.0, -1.0).astype(jnp.int32)
        offs_ref[e] = before.astype(jnp.int32)
        in_row = (cum - before).astype(BF16)
        ends = jnp.transpose(jnp.broadcast_to(cum[:, LANES - 1:LANES], cum.shape))[0:1, :]

        def slot_tile(jt, carry2):
            j = slot_iota + lax.convert_element_type(jt * sl, F32)
            done = ends <= j
            c_j = jnp.sum(jnp.where(done, 1.0, 0.0), axis=1, keepdims=True)
            before_j = jnp.max(jnp.where(done, ends, 0.0), axis=1, keepdims=True)
            pick = jnp.where(row_ids == c_j, 1.0, 0.0).astype(BF16)
            counts = jnp.dot(pick, in_row, preferred_element_type=F32)
            lane_j = jnp.sum(jnp.where(counts <= j - before_j, 1.0, 0.0), axis=1, keepdims=True)
            tok = c_j * float(LANES) + lane_j
            idx_ref[e, pl.ds(jt, 1), :] = jnp.transpose(jnp.broadcast_to(tok, (sl, LANES)))[0:1, :].astype(jnp.int32)
            return carry2

        lax.fori_loop(0, cap // sl, slot_tile, 0)
        return carry

    lax.fori_loop(0, N_EXPERTS, per_expert, 0)


def _topk(aff3, cap):
    e, nchunk, _ = aff3.shape
    return pl.pallas_call(
        functools.partial(_topk_kernel, cap=cap),
        out_shape=[jax.ShapeDtypeStruct((e, nchunk, LANES), jnp.int32),
                   jax.ShapeDtypeStruct((e, nchunk, LANES), jnp.int32),
                   jax.ShapeDtypeStruct((e, cap // TOPK_SL, TOPK_SL), jnp.int32)],
        compiler_params=pltpu.CompilerParams(vmem_limit_bytes=VMEM_LIMIT),
        name="expert_choice_topk",
    )(aff3)


def _ffn_kernel(idx_ref, tok_hbm, wg_ref, wu_ref, wd_ref, out_ref, xs_ref, sem_ref):
    e = pl.program_id(0)
    s = pl.program_id(1)
    n_steps = pl.num_programs(0) * pl.num_programs(1)
    step = e * pl.num_programs(1) + s
    ts = FFN_TS

    def gather_rows(step_, buf, r0, r1):
        for r in range(r0, r1):
            pltpu.make_async_copy(tok_hbm.at[pl.ds(idx_ref[step_ * ts + r], 1), :],
                                  xs_ref.at[buf, pl.ds(r, 1), :], sem_ref.at[buf]).start()

    def wait_rows(buf):
        pltpu.make_async_copy(tok_hbm.at[pl.ds(0, ts), :], xs_ref.at[buf], sem_ref.at[buf]).wait()

    @pl.when(step == 0)
    def _():
        gather_rows(0, 0, 0, ts)

    def tile(buf):
        wait_rows(buf)
        gather_rows(jnp.minimum(step + 1, n_steps - 1), 1 - buf, 0, ts)
        xs = xs_ref[buf, :, :D_MODEL].astype(BF16)
        aff = xs_ref[buf, :, D_MODEL:]
        lane = lax.broadcasted_iota(jnp.int32, aff.shape, 1)
        route = jnp.sum(jnp.where(lane == e, aff, 0.0), axis=1, keepdims=True)
        gate = jnp.dot(xs, wg_ref[0], preferred_element_type=F32)
        up = jnp.dot(xs, wu_ref[0], preferred_element_type=F32)
        hid = (gate * _sigmoid(gate) * up).astype(BF16)
        out_ref[0] = (jnp.dot(hid, wd_ref[0], preferred_element_type=F32) * route).astype(BF16)

        @pl.when(step == n_steps - 1)
        def _():
            wait_rows(1 - buf)

    for parity in range(2):
        pl.when(step % 2 == parity)(functools.partial(tile, parity))


def _ffn(idx, tok, wg, wu, wd, cap):
    n_tiles = cap // FFN_TS
    wmap = lambda e, s, *_: (e, 0, 0)
    return pl.pallas_call(
        _ffn_kernel,
        grid_spec=pltpu.PrefetchScalarGridSpec(
            num_scalar_prefetch=1,
            grid=(N_EXPERTS, n_tiles),
            in_specs=[pl.BlockSpec(memory_space=pl.ANY),
                      pl.BlockSpec((1, D_MODEL, EXPERT_HIDDEN), wmap),
                      pl.BlockSpec((1, D_MODEL, EXPERT_HIDDEN), wmap),
                      pl.BlockSpec((1, EXPERT_HIDDEN, D_MODEL), wmap)],
            out_specs=pl.BlockSpec((1, FFN_TS, D_MODEL), lambda e, s, *_: (e, s, 0)),
            scratch_shapes=[pltpu.VMEM((2, FFN_TS, TOK_ROW), F32), pltpu.SemaphoreType.DMA((2,))],
        ),
        out_shape=jax.ShapeDtypeStruct((N_EXPERTS, cap, D_MODEL), BF16),
        compiler_params=_params("arbitrary", "arbitrary"),
        name="expert_ffn",
    )(idx, tok, wg, wu, wd)


def _combine_kernel(base_ref, spos_ref, x1_ref, nf_ref, eo_hbm, o_ref, stage_ref, sem_ref, tail_ref, tail_sem,
                    acc_ref, *, cap):
    i = pl.program_id(0)
    tm, win = CMB_TM, CMB_WIN
    lane_ids = lax.broadcasted_iota(jnp.int32, (tm, win), 1)

    def window_start(tile, e):
        return jnp.minimum((base_ref[tile * N_EXPERTS + e] // CMB_ALIGN) * CMB_ALIGN, cap - win)

    def window_copy(tile, e, buf):
        a0 = pl.multiple_of(window_start(tile, e), CMB_ALIGN)
        return pltpu.make_async_copy(eo_hbm.at[e, pl.ds(a0, win), :], stage_ref.at[buf, pl.ds(e * win, win), :],
                                     sem_ref.at[buf, e])

    @pl.when(i == 0)
    def _():
        for e in range(N_EXPERTS):
            window_copy(0, e, 0).start()

    @pl.when(i + 1 < pl.num_programs(0))
    def _():
        for e in range(N_EXPERTS):
            window_copy(i + 1, e, (i + 1) % 2).start()

    buf = i % 2
    pieces = []
    for e in range(N_EXPERTS):
        window_copy(i, e, buf).wait()
        pieces.append(jnp.where(spos_ref[:, e:e + 1] == window_start(i, e) + lane_ids, 1.0, 0.0).astype(BF16))
    onehot = jnp.concatenate(pieces, axis=1)
    acc_ref[...] = x1_ref[...] + jnp.dot(onehot, stage_ref[buf], preferred_element_type=F32)

    for e in range(N_EXPERTS):
        end = base_ref[(i + 1) * N_EXPERTS + e]
        covered = window_start(i, e) + win

        @pl.when(end > covered)
        def _(e=e, end=end, covered=covered):
            def extra(w, carry):
                lo = covered + w * win
                a0 = pl.multiple_of(jnp.minimum(lo, cap - win), CMB_ALIGN)
                cp = pltpu.make_async_copy(eo_hbm.at[e, pl.ds(a0, win), :], tail_ref, tail_sem)
                cp.start()
                cp.wait()
                col = spos_ref[:, e:e + 1]
                hit = (col == a0 + lane_ids) & (col >= lo)
                acc_ref[...] += jnp.dot(jnp.where(hit, 1.0, 0.0).astype(BF16), tail_ref[...],
                                        preferred_element_type=F32)
                return carry

            lax.fori_loop(0, (end - covered + win - 1) // win, extra, 0)

    o_ref[...] = _rms_scale(acc_ref[...]) * nf_ref[...]


def _combine(base, spos_t, x1, nf, eo):
    n = x1.shape[0]
    cap = eo.shape[1]
    tm = CMB_TM
    row = lambda i, *_: (i, 0)
    return pl.pallas_call(
        functools.partial(_combine_kernel, cap=cap),
        grid_spec=pltpu.PrefetchScalarGridSpec(
            num_scalar_prefetch=1,
            grid=(n // tm,),
            in_specs=[pl.BlockSpec((tm, N_EXPERTS), row),
                      pl.BlockSpec((tm, D_MODEL), row), pl.BlockSpec((1, D_MODEL), lambda i, *_: (0, 0)),
                      pl.BlockSpec(memory_space=pl.ANY)],
            out_specs=pl.BlockSpec((tm, D_MODEL), row),
            scratch_shapes=[pltpu.VMEM((2, N_EXPERTS * CMB_WIN, D_MODEL), BF16),
                            pltpu.SemaphoreType.DMA((2, N_EXPERTS)),
                            pltpu.VMEM((CMB_WIN, D_MODEL), BF16), pltpu.SemaphoreType.DMA,
                            pltpu.VMEM((tm, D_MODEL), F32)],
        ),
        out_shape=jax.ShapeDtypeStruct((n, D_MODEL), F32),
        compiler_params=_params("arbitrary"),
        name="combine_final",
    )(base, spos_t, x1, nf, eo)


def _moe(x1, h2, aff_t, p):
    n = x1.shape[0]
    cap = EC_CAPACITY_FACTOR * n // N_EXPERTS
    spos, offs, idx = _topk(aff_t.reshape(N_EXPERTS, n // LANES, LANES), cap)
    eo = _ffn(idx.reshape(-1), h2, p["w_exp_gate"], p["w_exp_up"], p["w_exp_down"], cap)
    base = offs[:, ::CMB_TM // LANES, 0].T
    base = jnp.concatenate([base, jnp.full((1, N_EXPERTS), cap, jnp.int32)], axis=0).reshape(-1)
    return _combine(base, spos.reshape(N_EXPERTS, n).T, x1, p["norm_final"], eo)


def _pair_perm():
    base = np.concatenate([np.arange(0, 32), np.arange(64, 96), np.arange(32, 64), np.arange(96, 128)])
    return np.concatenate([base + LANES * b for b in range(AQ // LANES)])


def _rope_tables():
    pos = jnp.arange(SEQ, dtype=F32)
    ret_inv = 1.0 / (ROPE_THETA ** jnp.linspace(0.0, 1.0, RET_DK // 2, dtype=F32))
    ang = pos[:, None] * ret_inv[None, :]
    ret_cos = jnp.concatenate([jnp.cos(ang), jnp.cos(ang)], axis=-1)
    ret_sin = jnp.concatenate([-jnp.sin(ang), jnp.sin(ang)], axis=-1)
    att_inv = ROPE_THETA ** (-jnp.arange(0, ATT_DH, 2, dtype=F32) / ATT_DH)
    ang = pos[:, None] * att_inv[None, :]
    c, s = jnp.cos(ang), jnp.sin(ang)
    att_cos = jnp.concatenate([c, c, c, c], axis=-1)
    att_sin = jnp.concatenate([-s, -s, s, s], axis=-1)
    att_tabs = []
    for dil in ATT_DILATIONS:
        for tab in (att_cos, att_sin):
            att_tabs.append(tab.reshape(SEQ // dil, dil, LANES).transpose(1, 0, 2))
    return ret_cos, ret_sin, att_tabs


def _encoder(x, p):
    batch = x.shape[0]
    x2d = x.reshape(batch * SEQ, D_MODEL)
    rq, rk, rv, rg, gates = _proj_ret(x2d, p["norm_mix"], p["w_ret"], p["ret_cos"], p["ret_sin"])
    qkv = _proj_att(x2d, p["norm_mix"], p["w_att"], p["att_tabs"], batch)
    a = _retention(p["lgf"], p["lgb"], rq, rk, rv, rg, p["ret_norm"], batch)
    att = _attention(qkv, batch)
    x1, h2, aff_t = _mix_out(a, att, gates, x2d, p["w_ret_out"], p["w_att_out"], p["w_out"], p["norm_ffn"],
                             p["router_hi"], p["router_lo"])
    return _moe(x1, h2, aff_t, p).reshape(batch, SEQ, D_MODEL)


def kernel(x_prompt, x_sample, norm_mix, w_in, ret_log_decay_fwd, ret_log_decay_bwd, ret_norm, w_ret_out,
           w_att_out, w_out, norm_ffn, w_router, w_exp_gate, w_exp_up, w_exp_down, norm_final):
    w = w_in[0]
    c_rv = 2 * RQ
    c_rg = c_rv + RV
    c_aq = c_rg + RV
    c_ak = c_aq + AQ
    c_av = c_ak + AQ
    c_gate = c_av + AQ
    perm = _pair_perm()
    w_aq = w[:, c_aq:c_ak][:, perm] * (ATT_DH ** -0.5)
    w_ak = w[:, c_ak:c_av][:, perm]
    w_av = w[:, c_av:c_gate]
    ret_cos, ret_sin, att_tabs = _rope_tables()
    router_t = w_router[0].T
    router_hi = router_t.astype(BF16)
    p = {
        "norm_mix": norm_mix,
        "w_ret": jnp.concatenate([w[:, :c_aq], w[:, c_gate:]], axis=1).astype(BF16),
        "w_att": jnp.concatenate([w_aq, w_ak, w_av], axis=1).astype(BF16),
        "ret_cos": ret_cos, "ret_sin": ret_sin, "att_tabs": att_tabs,
        "lgf": ret_log_decay_fwd[0], "lgb": ret_log_decay_bwd[0],
        "ret_norm": ret_norm,
        "w_ret_out": w_ret_out[0].astype(BF16), "w_att_out": w_att_out[0].astype(BF16),
        "w_out": w_out[0].astype(BF16),
        "norm_ffn": norm_ffn,
        "router_hi": router_hi, "router_lo": (router_t - router_hi.astype(F32)).astype(BF16),
        "w_exp_gate": w_exp_gate[0].astype(BF16), "w_exp_up": w_exp_up[0].astype(BF16),
        "w_exp_down": w_exp_down[0].astype(BF16),
        "norm_final": norm_final.reshape(1, D_MODEL),
    }
    return _encoder(x_prompt, p), _encoder(x_sample, p)
```

```python
import functools

import numpy as np
import jax
import jax.numpy as jnp
from jax import lax
from jax.experimental import pallas as pl
from jax.experimental.pallas import tpu as pltpu

F32 = jnp.float32
BF16 = jnp.bfloat16

D_MODEL = 1024
SEQ = 4096
RET_HEADS = 4
RET_DK = 128
RET_DV = 256
RET_CHUNK = 256
RET_UNROLL = 4
RET_FIN = 512
ATT_DILATIONS = (1, 4, 16)
ATT_HALF_SPAN = 64
ATT_HEADS_PER_GROUP = 8
ATT_DH = 64
ATT_PAIRS = ATT_HEADS_PER_GROUP // 2
N_EXPERTS = 16
EC_CAPACITY_FACTOR = 2
EXPERT_HIDDEN = 2048
ROPE_THETA = 10000.0
EPS = 1e-6
RQ = RET_HEADS * RET_DK
RV = RET_HEADS * RET_DV
AQ = 3 * ATT_HEADS_PER_GROUP * ATT_DH
A_OUT = ATT_HEADS_PER_GROUP * ATT_DH
LANES = 128

PROJ_TM = 512
ATT_QB = 128
ATT_KW = ATT_QB + 2 * ATT_HALF_SPAN
ATT_UNROLL = 8
FFN_TS = 256
TOK_ROW = D_MODEL + LANES
TOPK_SL = 512
CMB_TM = 512
CMB_WIN = 128
CMB_ALIGN = 16
MASK_NEG = -1e30
VMEM_LIMIT = 56 * 1024 * 1024


def _params(*sem):
    return pltpu.CompilerParams(dimension_semantics=sem, vmem_limit_bytes=VMEM_LIMIT)


def _sigmoid(x):
    return 1.0 / (1.0 + jnp.exp(-x))


def _rms_scale(x):
    return x * lax.rsqrt(jnp.mean(x * x, axis=-1, keepdims=True) + EPS)


def _rope(a, cos, sin):
    return a * cos + pltpu.roll(a, 64, 1) * sin


def _proj_ret_kernel(x_ref, nm_ref, w_ref, cos_ref, sin_ref, rq_ref, rk_ref, rv_ref, rg_ref, g_ref):
    h = (_rms_scale(x_ref[...]) * nm_ref[...]).astype(BF16)
    cos = cos_ref[...]
    sin = sin_ref[...]

    def proj(c0, width):
        return jnp.dot(h, w_ref[:, c0:c0 + width], preferred_element_type=F32)

    acc = proj(0, RQ)
    for hh in range(RET_HEADS):
        sl = slice(hh * LANES, (hh + 1) * LANES)
        rq_ref[:, sl] = _rope(acc[:, sl], cos, sin).astype(BF16)
    acc = proj(RQ, RQ)
    for hh in range(RET_HEADS):
        sl = slice(hh * LANES, (hh + 1) * LANES)
        rk_ref[:, sl] = (_rope(acc[:, sl], cos, sin) * (RET_DK ** -0.5)).astype(BF16)
    for c in range(RV // 512):
        rv_ref[:, c * 512:(c + 1) * 512] = proj(2 * RQ + c * 512, 512).astype(BF16)
    for c in range(RV // 512):
        acc = proj(2 * RQ + RV + c * 512, 512)
        rg_ref[:, c * 512:(c + 1) * 512] = (acc * _sigmoid(acc)).astype(BF16)
    for c in range(2 * D_MODEL // 512):
        acc = proj(2 * RQ + 2 * RV + c * 512, 512)
        g_ref[:, c * 512:(c + 1) * 512] = _sigmoid(acc).astype(BF16)


def _proj_ret(x2d, nm, w, cos, sin):
    n = x2d.shape[0]
    nt = SEQ // PROJ_TM
    tm = PROJ_TM
    row = lambda i: (i, 0)
    fixed = lambda i: (0, 0)
    tab = lambda i: (i % nt, 0)
    return pl.pallas_call(
        _proj_ret_kernel,
        grid=(n // tm,),
        in_specs=[pl.BlockSpec((tm, D_MODEL), row), pl.BlockSpec((1, D_MODEL), fixed),
                  pl.BlockSpec(w.shape, fixed), pl.BlockSpec((tm, LANES), tab), pl.BlockSpec((tm, LANES), tab)],
        out_specs=[pl.BlockSpec((tm, RQ), row), pl.BlockSpec((tm, RQ), row), pl.BlockSpec((tm, RV), row),
                   pl.BlockSpec((tm, RV), row), pl.BlockSpec((tm, 2 * D_MODEL), row)],
        out_shape=[jax.ShapeDtypeStruct((n, RQ), BF16), jax.ShapeDtypeStruct((n, RQ), BF16),
                   jax.ShapeDtypeStruct((n, RV), BF16), jax.ShapeDtypeStruct((n, RV), BF16),
                   jax.ShapeDtypeStruct((n, 2 * D_MODEL), BF16)],
        compiler_params=_params("parallel"),
        name="proj_ret",
    )(x2d, nm, w, cos, sin)


def _proj_att_kernel(x_ref, nm_ref, w_ref, c0_ref, s0_ref, c1_ref, s1_ref, c2_ref, s2_ref, *rest):
    out_refs = rest[:9]
    scr_ref = rest[9]
    h = (_rms_scale(x_ref[...]) * nm_ref[...]).astype(BF16)
    tabs = ((c0_ref, s0_ref), (c1_ref, s1_ref), (c2_ref, s2_ref))
    tm = x_ref.shape[0]
    for sec in range(3):
        for g, dil in enumerate(ATT_DILATIONS):
            c0 = (sec * 3 + g) * A_OUT
            acc = jnp.dot(h, w_ref[:, c0:c0 + A_OUT], preferred_element_type=F32)
            out_ref = out_refs[sec * 3 + g]
            if dil > 1:
                for p in range(ATT_PAIRS):
                    scr_ref[p] = acc[:, p * LANES:(p + 1) * LANES]
            for r in range(dil):
                for p in range(ATT_PAIRS):
                    if dil == 1:
                        a = acc[:, p * LANES:(p + 1) * LANES]
                    else:
                        a = scr_ref[p, pl.ds(r, tm // dil, stride=dil), :]
                    if sec < 2:
                        a = _rope(a, tabs[g][0][r], tabs[g][1][r])
                    out_ref[0, p, r] = a.astype(BF16)


def _proj_att(x2d, nm, w, tabs, batch):
    n = x2d.shape[0]
    tm = PROJ_TM
    nt = SEQ // tm
    row = lambda i: (i, 0)
    fixed = lambda i: (0, 0)
    in_specs = [pl.BlockSpec((tm, D_MODEL), row), pl.BlockSpec((1, D_MODEL), fixed), pl.BlockSpec(w.shape, fixed)]
    for dil in ATT_DILATIONS:
        for _ in range(2):
            in_specs.append(pl.BlockSpec((dil, tm // dil, LANES), lambda i: (0, i % nt, 0)))
    out_specs, out_shape = [], []
    for _ in range(3):
        for dil in ATT_DILATIONS:
            out_specs.append(pl.BlockSpec((1, ATT_PAIRS, dil, tm // dil, LANES),
                                          lambda i: (i // nt, 0, 0, i % nt, 0)))
            out_shape.append(jax.ShapeDtypeStruct((batch, ATT_PAIRS, dil, SEQ // dil, LANES), BF16))
    return pl.pallas_call(
        _proj_att_kernel,
        grid=(n // tm,),
        in_specs=in_specs,
        out_specs=out_specs,
        out_shape=out_shape,
        scratch_shapes=[pltpu.VMEM((ATT_PAIRS, tm, LANES), F32)],
        compiler_params=_params("parallel"),
        name="proj_att",
    )(x2d, nm, w, *tabs)


def _retention_kernel(lgf_ref, lgb_ref, q_ref, k_ref, v_ref, rg_ref, nrm_ref, o_ref, part_ref, cross_ref):
    hd = pl.program_id(1)
    lgf = lgf_ref[hd]
    lgb = lgb_ref[hd]
    c = RET_CHUNK
    nc = q_ref.shape[0] // c
    ti = lax.broadcasted_iota(jnp.int32, (c, c), 0)
    si = lax.broadcasted_iota(jnp.int32, (c, c), 1)
    diff = (ti - si).astype(F32)
    fwd = diff >= 0.0
    decay = jnp.where(fwd, jnp.exp(lgf * jnp.where(fwd, diff, 0.0)), jnp.exp(lgb * jnp.where(fwd, 0.0, -diff)))
    idx = lax.broadcasted_iota(jnp.int32, (c, 1), 0).astype(F32)
    qf_dec = jnp.exp(lgf * (idx + 1.0))
    kf_dec = jnp.exp(lgf * (c - 1.0 - idx))
    qb_dec = jnp.exp(lgb * (c - idx))
    kb_dec = jnp.exp(lgb * idx)
    one = jnp.ones((1, 1), F32)
    cdf = jnp.exp(one * (lgf * c))
    cdb = jnp.exp(one * (lgb * c))
    nrm = nrm_ref[...]
    tn = (((0,), (0,)), ((), ()))
    nt = (((1,), (1,)), ((), ()))

    def step(j, states):
        for u in range(RET_UNROLL):
            states = one_chunk(j * RET_UNROLL + u, states)
        return states

    def one_chunk(i, states):
        sf, sb = states
        rows = pl.ds(pl.multiple_of(i * c, c), c)
        q = q_ref[rows, :]
        k = k_ref[rows, :]
        v = v_ref[rows, :]
        scores = lax.dot_general(q, k, nt, preferred_element_type=F32) * decay
        part = jnp.dot(scores.astype(BF16), v, preferred_element_type=F32)
        part += jnp.dot((q.astype(F32) * qf_dec).astype(BF16), sf.astype(BF16), preferred_element_type=F32)
        part_ref[rows, :] = part
        sf = sf * cdf + lax.dot_general((k.astype(F32) * kf_dec).astype(BF16), v, tn, preferred_element_type=F32)

        rows = pl.ds(pl.multiple_of((nc - 1 - i) * c, c), c)
        q = q_ref[rows, :].astype(F32)
        k = k_ref[rows, :].astype(F32)
        cross_ref[rows, :] = jnp.dot((q * qb_dec).astype(BF16), sb.astype(BF16), preferred_element_type=F32)
        sb = sb * cdb + lax.dot_general((k * kb_dec).astype(BF16), v_ref[rows, :], tn, preferred_element_type=F32)
        return sf, sb

    zero = jnp.zeros((RET_DK, RET_DV), F32)
    lax.fori_loop(0, nc // RET_UNROLL, step, (zero, zero))

    def finish(i, carry):
        rows = pl.ds(pl.multiple_of(i * RET_FIN, RET_FIN), RET_FIN)
        out = part_ref[rows, :] + cross_ref[rows, :]
        o_ref[rows, :] = (rg_ref[rows, :].astype(F32) * (_rms_scale(out) * nrm)).astype(BF16)
        return carry

    lax.fori_loop(0, q_ref.shape[0] // RET_FIN, finish, 0)


def _retention(lgf, lgb, rq, rk, rv, rg, nrm, batch):
    n = rq.shape[0]
    bh = lambda b, h, *_: (b, h)
    return pl.pallas_call(
        _retention_kernel,
        grid_spec=pltpu.PrefetchScalarGridSpec(
            num_scalar_prefetch=2,
            grid=(batch, RET_HEADS),
            in_specs=[pl.BlockSpec((SEQ, RET_DK), bh), pl.BlockSpec((SEQ, RET_DK), bh),
                      pl.BlockSpec((SEQ, RET_DV), bh), pl.BlockSpec((SEQ, RET_DV), bh),
                      pl.BlockSpec((1, RET_DV), lambda b, h, *_: (0, h))],
            out_specs=pl.BlockSpec((SEQ, RET_DV), bh),
            scratch_shapes=[pltpu.VMEM((SEQ, RET_DV), F32), pltpu.VMEM((SEQ, RET_DV), F32)],
        ),
        out_shape=jax.ShapeDtypeStruct((n, RV), BF16),
        compiler_params=_params("parallel", "parallel"),
        name="retention",
    )(lgf, lgb, rq, rk, rv, rg, nrm)


def _attn_block(q, kw, vw, bias):
    lane = lax.broadcasted_iota(jnp.int32, q.shape, 1)
    head0_qk = (lane < 32) | ((lane >= 64) & (lane < 96))
    outs, lses = [], []
    for hh in range(2):
        qm = jnp.where(head0_qk if hh == 0 else jnp.logical_not(head0_qk), q, jnp.zeros_like(q))
        s = lax.dot_general(qm, kw, (((1,), (1,)), ((), ())), preferred_element_type=F32) + bias
        m = jnp.max(s, axis=-1, keepdims=True)
        p = jnp.exp(s - m)
        l = jnp.sum(p, axis=-1, keepdims=True)
        o = jnp.dot(p.astype(BF16), vw, preferred_element_type=F32)
        outs.append(o / l)
        lses.append(m + jnp.log(l))
    head0_v = lane < ATT_DH
    return jnp.where(head0_v, outs[0], outs[1]), jnp.where(head0_v, lses[0], lses[1])


def _attention_kernel(q0, k0, v0, q1, k1, v1, q2, k2, v2, o_ref, o1_ref, l1_ref, o2_ref, l2_ref, bias_ref):
    qb, kw = ATT_QB, ATT_KW
    ji = lax.broadcasted_iota(jnp.int32, (qb, kw), 1) - lax.broadcasted_iota(jnp.int32, (qb, kw), 0)
    for n in range(3):
        rel = ji - n * ATT_HALF_SPAN
        bias_ref[n] = jnp.where((rel >= -ATT_HALF_SPAN) & (rel <= ATT_HALF_SPAN), 0.0, MASK_NEG).astype(F32)

    def group(q_ref, k_ref, v_ref, dil, finish):
        sub_len = SEQ // dil
        nblk = sub_len // qb

        def body(jt, carry):
            for u in range(ATT_UNROLL):
                it = jt * ATT_UNROLL + u
                r = it // nblk
                l0 = (it % nblk) * qb
                kstart = pl.multiple_of(jnp.clip(l0 - ATT_HALF_SPAN, 0, sub_len - kw), ATT_HALF_SPAN)
                bias = bias_ref[(l0 - kstart) // ATT_HALF_SPAN]
                q = q_ref[0, 0, r, pl.ds(pl.multiple_of(l0, qb), qb), :]
                kwin = k_ref[0, 0, r, pl.ds(kstart, kw), :]
                vwin = v_ref[0, 0, r, pl.ds(kstart, kw), :]
                o, lse = _attn_block(q, kwin, vwin, bias)
                finish(l0, r, o, lse)
            return carry

        lax.fori_loop(0, dil * nblk // ATT_UNROLL, body, 0)

    def store_strided(o_nat, l_nat, dil):
        def finish(l0, r, o, lse):
            rows = pl.ds(l0 * dil + r, ATT_QB, stride=dil)
            o_nat[rows, :] = o
            l_nat[rows, :] = lse
        return finish

    group(q2, k2, v2, ATT_DILATIONS[2], store_strided(o2_ref, l2_ref, ATT_DILATIONS[2]))
    group(q1, k1, v1, ATT_DILATIONS[1], store_strided(o1_ref, l1_ref, ATT_DILATIONS[1]))

    def merge(l0, r, o, lse):
        rows = pl.ds(pl.multiple_of(l0, ATT_QB), ATT_QB)
        la, lb = l1_ref[rows, :], l2_ref[rows, :]
        mx = jnp.maximum(lse, jnp.maximum(la, lb))
        w0, w1, w2 = jnp.exp(lse - mx), jnp.exp(la - mx), jnp.exp(lb - mx)
        num = w0 * o + w1 * o1_ref[rows, :] + w2 * o2_ref[rows, :]
        o_ref[rows, :] = (num / (w0 + w1 + w2)).astype(BF16)

    group(q0, k0, v0, ATT_DILATIONS[0], merge)


def _attention(qkv, batch):
    in_specs = []
    for g, dil in enumerate(ATT_DILATIONS):
        for _ in range(3):
            in_specs.append(pl.BlockSpec((1, 1, dil, SEQ // dil, LANES), lambda b, p: (b, p, 0, 0, 0)))
    args = []
    for g in range(3):
        args += [qkv[0 * 3 + g], qkv[1 * 3 + g], qkv[2 * 3 + g]]
    return pl.pallas_call(
        _attention_kernel,
        grid=(batch, ATT_PAIRS),
        in_specs=in_specs,
        out_specs=pl.BlockSpec((SEQ, LANES), lambda b, p: (b, p)),
        out_shape=jax.ShapeDtypeStruct((batch * SEQ, A_OUT), BF16),
        scratch_shapes=[pltpu.VMEM((SEQ, LANES), F32) for _ in range(4)]
        + [pltpu.VMEM((3, ATT_QB, ATT_KW), F32)],
        compiler_params=_params("parallel", "parallel"),
        name="dilated_attention",
    )(*args)


def _mix_out_kernel(a_ref, att_ref, g_ref, x_ref, wr_ref, wa_ref, wo_ref, nf_ref, rhi_ref, rlo_ref,
                    x1_ref, h2_ref, aff_ref):
    bra = jnp.dot(a_ref[...], wr_ref[...], preferred_element_type=F32)
    brb = jnp.dot(att_ref[...], wa_ref[...], preferred_element_type=F32)
    g = g_ref[...].astype(F32)
    merged = g[:, :D_MODEL] * bra + g[:, D_MODEL:] * brb
    x1 = x_ref[...] + jnp.dot(merged.astype(BF16), wo_ref[...], preferred_element_type=F32)
    x1_ref[...] = x1
    h2 = _rms_scale(x1) * nf_ref[...]
    hi = h2.astype(BF16)
    lo = (h2 - hi.astype(F32)).astype(BF16)
    nt = (((1,), (1,)), ((), ()))
    logits = (lax.dot_general(rhi_ref[...], hi, nt, preferred_element_type=F32)
              + lax.dot_general(rhi_ref[...], lo, nt, preferred_element_type=F32)
              + lax.dot_general(rlo_ref[...], hi, nt, preferred_element_type=F32))
    e = jnp.exp(logits - jnp.max(logits, axis=0, keepdims=True))
    aff = e / jnp.sum(e, axis=0, keepdims=True)
    aff_ref[...] = aff
    h2_ref[:, :D_MODEL] = h2
    pad = jnp.zeros((LANES - N_EXPERTS, aff.shape[1]), F32)
    h2_ref[:, D_MODEL:] = jnp.transpose(jnp.concatenate([aff, pad], axis=0))


def _mix_out(a, att, g, x2d, wr, wa, wo, nf, rhi, rlo):
    n = x2d.shape[0]
    tm = PROJ_TM
    row = lambda i: (i, 0)
    fixed = lambda i: (0, 0)
    return pl.pallas_call(
        _mix_out_kernel,
        grid=(n // tm,),
        in_specs=[pl.BlockSpec((tm, RV), row), pl.BlockSpec((tm, A_OUT), row), pl.BlockSpec((tm, 2 * D_MODEL), row),
                  pl.BlockSpec((tm, D_MODEL), row), pl.BlockSpec(wr.shape, fixed), pl.BlockSpec(wa.shape, fixed),
                  pl.BlockSpec(wo.shape, fixed), pl.BlockSpec((1, D_MODEL), fixed),
                  pl.BlockSpec(rhi.shape, fixed), pl.BlockSpec(rlo.shape, fixed)],
        out_specs=[pl.BlockSpec((tm, D_MODEL), row), pl.BlockSpec((tm, TOK_ROW), row),
                   pl.BlockSpec((N_EXPERTS, tm), lambda i: (0, i))],
        out_shape=[jax.ShapeDtypeStruct((n, D_MODEL), F32), jax.ShapeDtypeStruct((n, TOK_ROW), F32),
                   jax.ShapeDtypeStruct((N_EXPERTS, n), F32)],
        compiler_params=_params("parallel"),
        name="mix_out",
    )(a, att, g, x2d, wr, wa, wo, nf, rhi, rlo)


def _topk_kernel(aff_ref, spos_ref, offs_ref, idx_ref, *, cap):
    nchunk = aff_ref.shape[1]
    sl = TOPK_SL
    li = lax.broadcasted_iota(jnp.int32, (LANES, LANES), 0)
    lj = lax.broadcasted_iota(jnp.int32, (LANES, LANES), 1)
    upper = jnp.where(li <= lj, 1.0, 0.0).astype(BF16)
    ci = lax.broadcasted_iota(jnp.int32, (nchunk, nchunk), 0)
    cj = lax.broadcasted_iota(jnp.int32, (nchunk, nchunk), 1)
    lower = jnp.where(cj < ci, 1.0, 0.0).astype(BF16)
    row_ids = lax.broadcasted_iota(jnp.int32, (sl, nchunk), 1).astype(F32)
    slot_iota = lax.broadcasted_iota(jnp.int32, (sl, 1), 0).astype(F32)

    def total(x):
        return jnp.sum(jnp.sum(x, axis=0, keepdims=True), axis=1, keepdims=True)

    def cumsum(mask):
        within = jnp.dot(mask.astype(BF16), upper, preferred_element_type=F32)
        tot = jnp.broadcast_to(within[:, LANES - 1:LANES], within.shape).astype(BF16)
        before = jnp.dot(lower, tot, preferred_element_type=F32)
        return within + before, before

    def per_expert(e, carry):
        bits = pltpu.bitcast(aff_ref[e], jnp.int32)

        def search(i, thr):
            cand = thr | (1 << (30 - i))
            cnt = total(jnp.where(bits >= cand, 1, 0))
            return jnp.where(cnt >= cap, cand, thr)

        thr = lax.fori_loop(0, 31, search, jnp.zeros((1, 1), jnp.int32))
        gt = bits > thr
        eq = bits == thr
        need = (cap - total(jnp.where(gt, 1, 0))).astype(F32)
        eq_f = jnp.where(eq, 1.0, 0.0)
        eq_cum, _ = cumsum(eq_f)
        sel = gt | (eq & (eq_cum - eq_f < need))
        cum, before = cumsum(jnp.where(sel, 1.0, 0.0))
        spos_ref[e] = jnp.where(sel, cum - 1.0, -1.0).astype(jnp.int32)
        offs_ref[e] = before.astype(jnp.int32)
        in_row = (cum - before).astype(BF16)
        ends = jnp.transpose(jnp.broadcast_to(cum[:, LANES - 1:LANES], cum.shape))[0:1, :]

        def slot_tile(jt, carry2):
            j = slot_iota + lax.convert_element_type(jt * sl, F32)
            done = ends <= j
            c_j = jnp.sum(jnp.where(done, 1.0, 0.0), axis=1, keepdims=True)
            before_j = jnp.max(jnp.where(done, ends, 0.0), axis=1, keepdims=True)
            pick = jnp.where(row_ids == c_j, 1.0, 0.0).astype(BF16)
            counts = jnp.dot(pick, in_row, preferred_element_type=F32)
            lane_j = jnp.sum(jnp.where(counts <= j - before_j, 1.0, 0.0), axis=1, keepdims=True)
            tok = c_j * float(LANES) + lane_j
            idx_ref[e, pl.ds(jt, 1), :] = jnp.transpose(jnp.broadcast_to(tok, (sl, LANES)))[0:1, :].astype(jnp.int32)
            return carry2

        lax.fori_loop(0, cap // sl, slot_tile, 0)
        return carry

    lax.fori_loop(0, N_EXPERTS, per_expert, 0)


def _topk(aff3, cap):
    e, nchunk, _ = aff3.shape
    return pl.pallas_call(
        functools.partial(_topk_kernel, cap=cap),
        out_shape=[jax.ShapeDtypeStruct((e, nchunk, LANES), jnp.int32),
                   jax.ShapeDtypeStruct((e, nchunk, LANES), jnp.int32),
                   jax.ShapeDtypeStruct((e, cap // TOPK_SL, TOPK_SL), jnp.int32)],
        compiler_params=pltpu.CompilerParams(vmem_limit_bytes=VMEM_LIMIT),
        name="expert_choice_topk",
    )(aff3)


def _ffn_kernel(idx_ref, tok_hbm, wg_ref, wu_ref, wd_ref, out_ref, xs_ref, sem_ref):
    e = pl.program_id(0)
    s = pl.program_id(1)
    n_steps = pl.num_programs(0) * pl.num_programs(1)
    step = e * pl.num_programs(1) + s
    ts = FFN_TS

    def gather_rows(step_, buf, r0, r1):
        for r in range(r0, r1):
            pltpu.make_async_copy(tok_hbm.at[pl.ds(idx_ref[step_ * ts + r], 1), :],
                                  xs_ref.at[buf, pl.ds(r, 1), :], sem_ref.at[buf]).start()

    def wait_rows(buf):
        pltpu.make_async_copy(tok_hbm.at[pl.ds(0, ts), :], xs_ref.at[buf], sem_ref.at[buf]).wait()

    @pl.when(step == 0)
    def _():
        gather_rows(0, 0, 0, ts)

    def tile(buf):
        wait_rows(buf)
        gather_rows(jnp.minimum(step + 1, n_steps - 1), 1 - buf, 0, ts)
        xs = xs_ref[buf, :, :D_MODEL].astype(BF16)
        aff = xs_ref[buf, :, D_MODEL:]
        lane = lax.broadcasted_iota(jnp.int32, aff.shape, 1)
        route = jnp.sum(jnp.where(lane == e, aff, 0.0), axis=1, keepdims=True)
        gate = jnp.dot(xs, wg_ref[0], preferred_element_type=F32)
        up = jnp.dot(xs, wu_ref[0], preferred_element_type=F32)
        hid = (gate * _sigmoid(gate) * up).astype(BF16)
        out_ref[0] = (jnp.dot(hid, wd_ref[0], preferred_element_type=F32) * route).astype(BF16)

        @pl.when(step == n_steps - 1)
        def _():
            wait_rows(1 - buf)

    for parity in range(2):
        pl.when(step % 2 == parity)(functools.partial(tile, parity))


def _ffn(idx, tok, wg, wu, wd, cap):
    n_tiles = cap // FFN_TS
    wmap = lambda e, s, *_: (e, 0, 0)
    return pl.pallas_call(
        _ffn_kernel,
        grid_spec=pltpu.PrefetchScalarGridSpec(
            num_scalar_prefetch=1,
            grid=(N_EXPERTS, n_tiles),
            in_specs=[pl.BlockSpec(memory_space=pl.ANY),
                      pl.BlockSpec((1, D_MODEL, EXPERT_HIDDEN), wmap),
                      pl.BlockSpec((1, D_MODEL, EXPERT_HIDDEN), wmap),
                      pl.BlockSpec((1, EXPERT_HIDDEN, D_MODEL), wmap)],
            out_specs=pl.BlockSpec((1, FFN_TS, D_MODEL), lambda e, s, *_: (e, s, 0)),
            scratch_shapes=[pltpu.VMEM((2, FFN_TS, TOK_ROW), F32), pltpu.SemaphoreType.DMA((2,))],
        ),
        out_shape=jax.ShapeDtypeStruct((N_EXPERTS, cap, D_MODEL), BF16),
        compiler_params=_params("arbitrary", "arbitrary"),
        name="expert_ffn",
    )(idx, tok, wg, wu, wd)


def _combine_kernel(base_ref, spos_ref, x1_ref, nf_ref, eo_hbm, o_ref, stage_ref, sem_ref, tail_ref, tail_sem,
                    acc_ref, *, cap):
    i = pl.program_id(0)
    tm, win = CMB_TM, CMB_WIN
    row_ids = lax.broadcasted_iota(jnp.int32, (win, tm), 0)
    tn = (((0,), (0,)), ((), ()))

    def window_start(tile, e):
        return jnp.minimum((base_ref[tile * N_EXPERTS + e] // CMB_ALIGN) * CMB_ALIGN, cap - win)

    def window_copy(tile, e, buf):
        a0 = pl.multiple_of(window_start(tile, e), CMB_ALIGN)
        return pltpu.make_async_copy(eo_hbm.at[e, pl.ds(a0, win), :], stage_ref.at[buf, pl.ds(e * win, win), :],
                                     sem_ref.at[buf, e])

    @pl.when(i == 0)
    def _():
        for e in range(N_EXPERTS):
            window_copy(0, e, 0).start()

    @pl.when(i + 1 < pl.num_programs(0))
    def _():
        for e in range(N_EXPERTS):
            window_copy(i + 1, e, (i + 1) % 2).start()

    buf = i % 2
    pieces = []
    for e in range(N_EXPERTS):
        window_copy(i, e, buf).wait()
        pieces.append(jnp.where(spos_ref[e:e + 1, :] == window_start(i, e) + row_ids, 1.0, 0.0).astype(BF16))
    pick = jnp.concatenate(pieces, axis=0)
    acc_ref[...] = x1_ref[...] + lax.dot_general(pick, stage_ref[buf], tn, preferred_element_type=F32)

    for e in range(N_EXPERTS):
        end = base_ref[(i + 1) * N_EXPERTS + e]
        covered = window_start(i, e) + win

        @pl.when(end > covered)
        def _(e=e, end=end, covered=covered):
            def extra(w, carry):
                lo = covered + w * win
                a0 = pl.multiple_of(jnp.minimum(lo, cap - win), CMB_ALIGN)
                cp = pltpu.make_async_copy(eo_hbm.at[e, pl.ds(a0, win), :], tail_ref, tail_sem)
                cp.start()
                cp.wait()
                pos = spos_ref[e:e + 1, :]
                hit = (pos == a0 + row_ids) & (pos >= lo)
                acc_ref[...] += lax.dot_general(jnp.where(hit, 1.0, 0.0).astype(BF16), tail_ref[...], tn,
                                                preferred_element_type=F32)
                return carry

            lax.fori_loop(0, (end - covered + win - 1) // win, extra, 0)

    o_ref[...] = _rms_scale(acc_ref[...]) * nf_ref[...]


def _combine(base, spos, x1, nf, eo):
    n = x1.shape[0]
    cap = eo.shape[1]
    tm = CMB_TM
    row = lambda i, *_: (i, 0)
    return pl.pallas_call(
        functools.partial(_combine_kernel, cap=cap),
        grid_spec=pltpu.PrefetchScalarGridSpec(
            num_scalar_prefetch=1,
            grid=(n // tm,),
            in_specs=[pl.BlockSpec((N_EXPERTS, tm), lambda i, *_: (0, i)),
                      pl.BlockSpec((tm, D_MODEL), row), pl.BlockSpec((1, D_MODEL), lambda i, *_: (0, 0)),
                      pl.BlockSpec(memory_space=pl.ANY)],
            out_specs=pl.BlockSpec((tm, D_MODEL), row),
            scratch_shapes=[pltpu.VMEM((2, N_EXPERTS * CMB_WIN, D_MODEL), BF16),
                            pltpu.SemaphoreType.DMA((2, N_EXPERTS)),
                            pltpu.VMEM((CMB_WIN, D_MODEL), BF16), pltpu.SemaphoreType.DMA,
                            pltpu.VMEM((tm, D_MODEL), F32)],
        ),
        out_shape=jax.ShapeDtypeStruct((n, D_MODEL), F32),
        compiler_params=_params("arbitrary"),
        name="combine_final",
    )(base, spos, x1, nf, eo)


def _moe(x1, h2, aff_t, p):
    n = x1.shape[0]
    cap = EC_CAPACITY_FACTOR * n // N_EXPERTS
    spos, offs, idx = _topk(aff_t.reshape(N_EXPERTS, n // LANES, LANES), cap)
    eo = _ffn(idx.reshape(-1), h2, p["w_exp_gate"], p["w_exp_up"], p["w_exp_down"], cap)
    base = offs[:, ::CMB_TM // LANES, 0].T
    base = jnp.concatenate([base, jnp.full((1, N_EXPERTS), cap, jnp.int32)], axis=0).reshape(-1)
    return _combine(base, spos.reshape(N_EXPERTS, n), x1, p["norm_final"], eo)


def _pair_perm():
    base = np.concatenate([np.arange(0, 32), np.arange(64, 96), np.arange(32, 64), np.arange(96, 128)])
    return np.concatenate([base + LANES * b for b in range(AQ // LANES)])


def _rope_tables():
    pos = jnp.arange(SEQ, dtype=F32)
    ret_inv = 1.0 / (ROPE_THETA ** jnp.linspace(0.0, 1.0, RET_DK // 2, dtype=F32))
    ang = pos[:, None] * ret_inv[None, :]
    ret_cos = jnp.concatenate([jnp.cos(ang), jnp.cos(ang)], axis=-1)
    ret_sin = jnp.concatenate([-jnp.sin(ang), jnp.sin(ang)], axis=-1)
    att_inv = ROPE_THETA ** (-jnp.arange(0, ATT_DH, 2, dtype=F32) / ATT_DH)
    ang = pos[:, None] * att_inv[None, :]
    c, s = jnp.cos(ang), jnp.sin(ang)
    att_cos = jnp.concatenate([c, c, c, c], axis=-1)
    att_sin = jnp.concatenate([-s, -s, s, s], axis=-1)
    att_tabs = []
    for dil in ATT_DILATIONS:
        for tab in (att_cos, att_sin):
            att_tabs.append(tab.reshape(SEQ // dil, dil, LANES).transpose(1, 0, 2))
    return ret_cos, ret_sin, att_tabs


def _encoder(x, p):
    batch = x.shape[0]
    x2d = x.reshape(batch * SEQ, D_MODEL)
    rq, rk, rv, rg, gates = _proj_ret(x2d, p["norm_mix"], p["w_ret"], p["ret_cos"], p["ret_sin"])
    qkv = _proj_att(x2d, p["norm_mix"], p["w_att"], p["att_tabs"], batch)
    a = _retention(p["lgf"], p["lgb"], rq, rk, rv, rg, p["ret_norm"], batch)
    att = _attention(qkv, batch)
    x1, h2, aff_t = _mix_out(a, att, gates, x2d, p["w_ret_out"], p["w_att_out"], p["w_out"], p["norm_ffn"],
                             p["router_hi"], p["router_lo"])
    return _moe(x1, h2, aff_t, p).reshape(batch, SEQ, D_MODEL)


def kernel(x_prompt, x_sample, norm_mix, w_in, ret_log_decay_fwd, ret_log_decay_bwd, ret_norm, w_ret_out,
           w_att_out, w_out, norm_ffn, w_router, w_exp_gate, w_exp_up, w_exp_down, norm_final):
    w = w_in[0]
    c_rv = 2 * RQ
    c_rg = c_rv + RV
    c_aq = c_rg + RV
    c_ak = c_aq + AQ
    c_av = c_ak + AQ
    c_gate = c_av + AQ
    perm = _pair_perm()
    w_aq = w[:, c_aq:c_ak][:, perm] * (ATT_DH ** -0.5)
    w_ak = w[:, c_ak:c_av][:, perm]
    w_av = w[:, c_av:c_gate]
    ret_cos, ret_sin, att_tabs = _rope_tables()
    router_t = w_router[0].T
    router_hi = router_t.astype(BF16)
    p = {
        "norm_mix": norm_mix,
        "w_ret": jnp.concatenate([w[:, :c_aq], w[:, c_gate:]], axis=1).astype(BF16),
        "w_att": jnp.concatenate([w_aq, w_ak, w_av], axis=1).astype(BF16),
        "ret_cos": ret_cos, "ret_sin": ret_sin, "att_tabs": att_tabs,
        "lgf": ret_log_decay_fwd[0], "lgb": ret_log_decay_bwd[0],
        "ret_norm": ret_norm,
        "w_ret_out": w_ret_out[0].astype(BF16), "w_att_out": w_att_out[0].astype(BF16),
        "w_out": w_out[0].astype(BF16),
        "norm_ffn": norm_ffn,
        "router_hi": router_hi, "router_lo": (router_t - router_hi.astype(F32)).astype(BF16),
        "w_exp_gate": w_exp_gate[0].astype(BF16), "w_exp_up": w_exp_up[0].astype(BF16),
        "w_exp_down": w_exp_down[0].astype(BF16),
        "norm_final": norm_final.reshape(1, D_MODEL),
    }
    return _encoder(x_prompt, p), _encoder(x_sample, p)
```

```python
import functools

import numpy as np
import jax
import jax.numpy as jnp
from jax import lax
from jax.experimental import pallas as pl
from jax.experimental.pallas import tpu as pltpu

F32 = jnp.float32
BF16 = jnp.bfloat16

D_MODEL = 1024
SEQ = 4096
RET_HEADS = 4
RET_DK = 128
RET_DV = 256
RET_CHUNK = 256
RET_UNROLL = 4
RET_FIN = 512
ATT_DILATIONS = (1, 4, 16)
ATT_HALF_SPAN = 64
ATT_HEADS_PER_GROUP = 8
ATT_DH = 64
ATT_PAIRS = ATT_HEADS_PER_GROUP // 2
N_EXPERTS = 16
EC_CAPACITY_FACTOR = 2
EXPERT_HIDDEN = 2048
ROPE_THETA = 10000.0
EPS = 1e-6
RQ = RET_HEADS * RET_DK
RV = RET_HEADS * RET_DV
AQ = 3 * ATT_HEADS_PER_GROUP * ATT_DH
A_OUT = ATT_HEADS_PER_GROUP * ATT_DH
LANES = 128

PROJ_TM = 512
ATT_QB = 128
ATT_KW = ATT_QB + 2 * ATT_HALF_SPAN
ATT_UNROLL = 8
FFN_TS = 256
TOK_ROW = D_MODEL + LANES
TOPK_SL = 512
CMB_TM = 512
CMB_WIN = 112
CMB_ALIGN = 16
MASK_NEG = -1e30
VMEM_LIMIT = 56 * 1024 * 1024


def _params(*sem):
    return pltpu.CompilerParams(dimension_semantics=sem, vmem_limit_bytes=VMEM_LIMIT)


def _sigmoid(x):
    return 1.0 / (1.0 + jnp.exp(-x))


def _rms_scale(x):
    return x * lax.rsqrt(jnp.mean(x * x, axis=-1, keepdims=True) + EPS)


def _rope(a, cos, sin):
    return a * cos + pltpu.roll(a, 64, 1) * sin


def _proj_ret_kernel(x_ref, nm_ref, w_ref, cos_ref, sin_ref, rq_ref, rk_ref, rv_ref, rg_ref, g_ref):
    h = (_rms_scale(x_ref[...]) * nm_ref[...]).astype(BF16)
    cos = cos_ref[...]
    sin = sin_ref[...]

    def proj(c0, width):
        return jnp.dot(h, w_ref[:, c0:c0 + width], preferred_element_type=F32)

    acc = proj(0, RQ)
    for hh in range(RET_HEADS):
        sl = slice(hh * LANES, (hh + 1) * LANES)
        rq_ref[:, sl] = _rope(acc[:, sl], cos, sin).astype(BF16)
    acc = proj(RQ, RQ)
    for hh in range(RET_HEADS):
        sl = slice(hh * LANES, (hh + 1) * LANES)
        rk_ref[:, sl] = (_rope(acc[:, sl], cos, sin) * (RET_DK ** -0.5)).astype(BF16)
    for c in range(RV // 512):
        rv_ref[:, c * 512:(c + 1) * 512] = proj(2 * RQ + c * 512, 512).astype(BF16)
    for c in range(RV // 512):
        acc = proj(2 * RQ + RV + c * 512, 512)
        rg_ref[:, c * 512:(c + 1) * 512] = (acc * _sigmoid(acc)).astype(BF16)
    for c in range(2 * D_MODEL // 512):
        acc = proj(2 * RQ + 2 * RV + c * 512, 512)
        g_ref[:, c * 512:(c + 1) * 512] = _sigmoid(acc).astype(BF16)


def _proj_ret(x2d, nm, w, cos, sin):
    n = x2d.shape[0]
    nt = SEQ // PROJ_TM
    tm = PROJ_TM
    row = lambda i: (i, 0)
    fixed = lambda i: (0, 0)
    tab = lambda i: (i % nt, 0)
    return pl.pallas_call(
        _proj_ret_kernel,
        grid=(n // tm,),
        in_specs=[pl.BlockSpec((tm, D_MODEL), row), pl.BlockSpec((1, D_MODEL), fixed),
                  pl.BlockSpec(w.shape, fixed), pl.BlockSpec((tm, LANES), tab), pl.BlockSpec((tm, LANES), tab)],
        out_specs=[pl.BlockSpec((tm, RQ), row), pl.BlockSpec((tm, RQ), row), pl.BlockSpec((tm, RV), row),
                   pl.BlockSpec((tm, RV), row), pl.BlockSpec((tm, 2 * D_MODEL), row)],
        out_shape=[jax.ShapeDtypeStruct((n, RQ), BF16), jax.ShapeDtypeStruct((n, RQ), BF16),
                   jax.ShapeDtypeStruct((n, RV), BF16), jax.ShapeDtypeStruct((n, RV), BF16),
                   jax.ShapeDtypeStruct((n, 2 * D_MODEL), BF16)],
        compiler_params=_params("parallel"),
        name="proj_ret",
    )(x2d, nm, w, cos, sin)


def _proj_att_kernel(x_ref, nm_ref, w_ref, c0_ref, s0_ref, c1_ref, s1_ref, c2_ref, s2_ref, *rest):
    out_refs = rest[:9]
    scr_ref = rest[9]
    h = (_rms_scale(x_ref[...]) * nm_ref[...]).astype(BF16)
    tabs = ((c0_ref, s0_ref), (c1_ref, s1_ref), (c2_ref, s2_ref))
    tm = x_ref.shape[0]
    for sec in range(3):
        for g, dil in enumerate(ATT_DILATIONS):
            c0 = (sec * 3 + g) * A_OUT
            acc = jnp.dot(h, w_ref[:, c0:c0 + A_OUT], preferred_element_type=F32)
            out_ref = out_refs[sec * 3 + g]
            if dil > 1:
                for p in range(ATT_PAIRS):
                    scr_ref[p] = acc[:, p * LANES:(p + 1) * LANES]
            for r in range(dil):
                for p in range(ATT_PAIRS):
                    if dil == 1:
                        a = acc[:, p * LANES:(p + 1) * LANES]
                    else:
                        a = scr_ref[p, pl.ds(r, tm // dil, stride=dil), :]
                    if sec < 2:
                        a = _rope(a, tabs[g][0][r], tabs[g][1][r])
                    out_ref[0, p, r] = a.astype(BF16)


def _proj_att(x2d, nm, w, tabs, batch):
    n = x2d.shape[0]
    tm = PROJ_TM
    nt = SEQ // tm
    row = lambda i: (i, 0)
    fixed = lambda i: (0, 0)
    in_specs = [pl.BlockSpec((tm, D_MODEL), row), pl.BlockSpec((1, D_MODEL), fixed), pl.BlockSpec(w.shape, fixed)]
    for dil in ATT_DILATIONS:
        for _ in range(2):
            in_specs.append(pl.BlockSpec((dil, tm // dil, LANES), lambda i: (0, i % nt, 0)))
    out_specs, out_shape = [], []
    for _ in range(3):
        for dil in ATT_DILATIONS:
            out_specs.append(pl.BlockSpec((1, ATT_PAIRS, dil, tm // dil, LANES),
                                          lambda i: (i // nt, 0, 0, i % nt, 0)))
            out_shape.append(jax.ShapeDtypeStruct((batch, ATT_PAIRS, dil, SEQ // dil, LANES), BF16))
    return pl.pallas_call(
        _proj_att_kernel,
        grid=(n // tm,),
        in_specs=in_specs,
        out_specs=out_specs,
        out_shape=out_shape,
        scratch_shapes=[pltpu.VMEM((ATT_PAIRS, tm, LANES), F32)],
        compiler_params=_params("parallel"),
        name="proj_att",
    )(x2d, nm, w, *tabs)


def _retention_kernel(lgf_ref, lgb_ref, q_ref, k_ref, v_ref, rg_ref, nrm_ref, o_ref, part_ref, cross_ref):
    hd = pl.program_id(1)
    lgf = lgf_ref[hd]
    lgb = lgb_ref[hd]
    c = RET_CHUNK
    nc = q_ref.shape[0] // c
    ti = lax.broadcasted_iota(jnp.int32, (c, c), 0)
    si = lax.broadcasted_iota(jnp.int32, (c, c), 1)
    diff = (ti - si).astype(F32)
    fwd = diff >= 0.0
    decay = jnp.where(fwd, jnp.exp(lgf * jnp.where(fwd, diff, 0.0)), jnp.exp(lgb * jnp.where(fwd, 0.0, -diff)))
    idx = lax.broadcasted_iota(jnp.int32, (c, 1), 0).astype(F32)
    qf_dec = jnp.exp(lgf * (idx + 1.0))
    kf_dec = jnp.exp(lgf * (c - 1.0 - idx))
    qb_dec = jnp.exp(lgb * (c - idx))
    kb_dec = jnp.exp(lgb * idx)
    one = jnp.ones((1, 1), F32)
    cdf = jnp.exp(one * (lgf * c))
    cdb = jnp.exp(one * (lgb * c))
    nrm = nrm_ref[...]
    tn = (((0,), (0,)), ((), ()))
    nt = (((1,), (1,)), ((), ()))

    def step(j, states):
        for u in range(RET_UNROLL):
            states = one_chunk(j * RET_UNROLL + u, states)
        return states

    def one_chunk(i, states):
        sf, sb = states
        rows = pl.ds(pl.multiple_of(i * c, c), c)
        q = q_ref[rows, :]
        k = k_ref[rows, :]
        v = v_ref[rows, :]
        scores = lax.dot_general(q, k, nt, preferred_element_type=F32) * decay
        part = jnp.dot(scores.astype(BF16), v, preferred_element_type=F32)
        part += jnp.dot((q.astype(F32) * qf_dec).astype(BF16), sf.astype(BF16), preferred_element_type=F32)
        part_ref[rows, :] = part
        sf = sf * cdf + lax.dot_general((k.astype(F32) * kf_dec).astype(BF16), v, tn, preferred_element_type=F32)

        rows = pl.ds(pl.multiple_of((nc - 1 - i) * c, c), c)
        q = q_ref[rows, :].astype(F32)
        k = k_ref[rows, :].astype(F32)
        cross_ref[rows, :] = jnp.dot((q * qb_dec).astype(BF16), sb.astype(BF16), preferred_element_type=F32)
        sb = sb * cdb + lax.dot_general((k * kb_dec).astype(BF16), v_ref[rows, :], tn, preferred_element_type=F32)
        return sf, sb

    zero = jnp.zeros((RET_DK, RET_DV), F32)
    lax.fori_loop(0, nc // RET_UNROLL, step, (zero, zero))

    def finish(i, carry):
        rows = pl.ds(pl.multiple_of(i * RET_FIN, RET_FIN), RET_FIN)
        out = part_ref[rows, :] + cross_ref[rows, :]
        o_ref[rows, :] = (rg_ref[rows, :].astype(F32) * (_rms_scale(out) * nrm)).astype(BF16)
        return carry

    lax.fori_loop(0, q_ref.shape[0] // RET_FIN, finish, 0)


def _retention(lgf, lgb, rq, rk, rv, rg, nrm, batch):
    n = rq.shape[0]
    bh = lambda b, h, *_: (b, h)
    return pl.pallas_call(
        _retention_kernel,
        grid_spec=pltpu.PrefetchScalarGridSpec(
            num_scalar_prefetch=2,
            grid=(batch, RET_HEADS),
            in_specs=[pl.BlockSpec((SEQ, RET_DK), bh), pl.BlockSpec((SEQ, RET_DK), bh),
                      pl.BlockSpec((SEQ, RET_DV), bh), pl.BlockSpec((SEQ, RET_DV), bh),
                      pl.BlockSpec((1, RET_DV), lambda b, h, *_: (0, h))],
            out_specs=pl.BlockSpec((SEQ, RET_DV), bh),
            scratch_shapes=[pltpu.VMEM((SEQ, RET_DV), F32), pltpu.VMEM((SEQ, RET_DV), F32)],
        ),
        out_shape=jax.ShapeDtypeStruct((n, RV), BF16),
        compiler_params=_params("parallel", "parallel"),
        name="retention",
    )(lgf, lgb, rq, rk, rv, rg, nrm)


def _attn_block(q, kw, vw, bias):
    lane = lax.broadcasted_iota(jnp.int32, q.shape, 1)
    head0_qk = (lane < 32) | ((lane >= 64) & (lane < 96))
    outs, maxs, sums = [], [], []
    for hh in range(2):
        qm = jnp.where(head0_qk if hh == 0 else jnp.logical_not(head0_qk), q, jnp.zeros_like(q))
        s = lax.dot_general(qm, kw, (((1,), (1,)), ((), ())), preferred_element_type=F32) + bias
        m = jnp.max(s, axis=-1, keepdims=True)
        p = jnp.exp(s - m)
        outs.append(jnp.dot(p.astype(BF16), vw, preferred_element_type=F32))
        maxs.append(m)
        sums.append(jnp.sum(p, axis=-1, keepdims=True))
    head0_v = lane < ATT_DH
    return (jnp.where(head0_v, outs[0], outs[1]), jnp.where(head0_v, maxs[0], maxs[1]),
            jnp.where(head0_v, sums[0], sums[1]))


def _attention_kernel(q0, k0, v0, q1, k1, v1, q2, k2, v2, o_ref, o1_ref, m1_ref, l1_ref, o2_ref, m2_ref, l2_ref,
                      bias_ref):
    qb, kw = ATT_QB, ATT_KW
    ji = lax.broadcasted_iota(jnp.int32, (qb, kw), 1) - lax.broadcasted_iota(jnp.int32, (qb, kw), 0)
    for n in range(3):
        rel = ji - n * ATT_HALF_SPAN
        bias_ref[n] = jnp.where((rel >= -ATT_HALF_SPAN) & (rel <= ATT_HALF_SPAN), 0.0, MASK_NEG).astype(F32)

    def group(q_ref, k_ref, v_ref, dil, finish):
        sub_len = SEQ // dil
        nblk = sub_len // qb

        def body(jt, carry):
            for u in range(ATT_UNROLL):
                it = jt * ATT_UNROLL + u
                r = it // nblk
                l0 = (it % nblk) * qb
                kstart = pl.multiple_of(jnp.clip(l0 - ATT_HALF_SPAN, 0, sub_len - kw), ATT_HALF_SPAN)
                bias = bias_ref[(l0 - kstart) // ATT_HALF_SPAN]
                q = q_ref[0, 0, r, pl.ds(pl.multiple_of(l0, qb), qb), :]
                kwin = k_ref[0, 0, r, pl.ds(kstart, kw), :]
                vwin = v_ref[0, 0, r, pl.ds(kstart, kw), :]
                finish(l0, r, *_attn_block(q, kwin, vwin, bias))
            return carry

        lax.fori_loop(0, dil * nblk // ATT_UNROLL, body, 0)

    def store_strided(o_nat, m_nat, l_nat, dil):
        def finish(l0, r, o, m, l):
            rows = pl.ds(l0 * dil + r, ATT_QB, stride=dil)
            o_nat[rows, :] = o
            m_nat[rows, :] = m
            l_nat[rows, :] = l
        return finish

    group(q2, k2, v2, ATT_DILATIONS[2], store_strided(o2_ref, m2_ref, l2_ref, ATT_DILATIONS[2]))
    group(q1, k1, v1, ATT_DILATIONS[1], store_strided(o1_ref, m1_ref, l1_ref, ATT_DILATIONS[1]))

    def merge(l0, r, o, m, l):
        rows = pl.ds(pl.multiple_of(l0, ATT_QB), ATT_QB)
        ma, mb = m1_ref[rows, :], m2_ref[rows, :]
        mx = jnp.maximum(m, jnp.maximum(ma, mb))
        w0, w1, w2 = jnp.exp(m - mx), jnp.exp(ma - mx), jnp.exp(mb - mx)
        num = w0 * o + w1 * o1_ref[rows, :] + w2 * o2_ref[rows, :]
        den = w0 * l + w1 * l1_ref[rows, :] + w2 * l2_ref[rows, :]
        o_ref[rows, :] = (num / den).astype(BF16)

    group(q0, k0, v0, ATT_DILATIONS[0], merge)


def _attention(qkv, batch):
    in_specs = []
    for g, dil in enumerate(ATT_DILATIONS):
        for _ in range(3):
            in_specs.append(pl.BlockSpec((1, 1, dil, SEQ // dil, LANES), lambda b, p: (b, p, 0, 0, 0)))
    args = []
    for g in range(3):
        args += [qkv[0 * 3 + g], qkv[1 * 3 + g], qkv[2 * 3 + g]]
    return pl.pallas_call(
        _attention_kernel,
        grid=(batch, ATT_PAIRS),
        in_specs=in_specs,
        out_specs=pl.BlockSpec((SEQ, LANES), lambda b, p: (b, p)),
        out_shape=jax.ShapeDtypeStruct((batch * SEQ, A_OUT), BF16),
        scratch_shapes=[pltpu.VMEM((SEQ, LANES), F32) for _ in range(6)]
        + [pltpu.VMEM((3, ATT_QB, ATT_KW), F32)],
        compiler_params=_params("parallel", "parallel"),
        name="dilated_attention",
    )(*args)


def _mix_out_kernel(a_ref, att_ref, g_ref, x_ref, wr_ref, wa_ref, wo_ref, nf_ref, rhi_ref, rlo_ref,
                    x1_ref, h2_ref, aff_ref):
    bra = jnp.dot(a_ref[...], wr_ref[...], preferred_element_type=F32)
    brb = jnp.dot(att_ref[...], wa_ref[...], preferred_element_type=F32)
    g = g_ref[...].astype(F32)
    merged = g[:, :D_MODEL] * bra + g[:, D_MODEL:] * brb
    x1 = x_ref[...] + jnp.dot(merged.astype(BF16), wo_ref[...], preferred_element_type=F32)
    x1_ref[...] = x1
    h2 = _rms_scale(x1) * nf_ref[...]
    hi = h2.astype(BF16)
    lo = (h2 - hi.astype(F32)).astype(BF16)
    nt = (((1,), (1,)), ((), ()))
    logits = (lax.dot_general(rhi_ref[...], hi, nt, preferred_element_type=F32)
              + lax.dot_general(rhi_ref[...], lo, nt, preferred_element_type=F32)
              + lax.dot_general(rlo_ref[...], hi, nt, preferred_element_type=F32))
    e = jnp.exp(logits - jnp.max(logits, axis=0, keepdims=True))
    aff = e / jnp.sum(e, axis=0, keepdims=True)
    aff_ref[...] = aff
    h2_ref[:, :D_MODEL] = h2
    pad = jnp.zeros((LANES - N_EXPERTS, aff.shape[1]), F32)
    h2_ref[:, D_MODEL:] = jnp.transpose(jnp.concatenate([aff, pad], axis=0))


def _mix_out(a, att, g, x2d, wr, wa, wo, nf, rhi, rlo):
    n = x2d.shape[0]
    tm = PROJ_TM
    row = lambda i: (i, 0)
    fixed = lambda i: (0, 0)
    return pl.pallas_call(
        _mix_out_kernel,
        grid=(n // tm,),
        in_specs=[pl.BlockSpec((tm, RV), row), pl.BlockSpec((tm, A_OUT), row), pl.BlockSpec((tm, 2 * D_MODEL), row),
                  pl.BlockSpec((tm, D_MODEL), row), pl.BlockSpec(wr.shape, fixed), pl.BlockSpec(wa.shape, fixed),
                  pl.BlockSpec(wo.shape, fixed), pl.BlockSpec((1, D_MODEL), fixed),
                  pl.BlockSpec(rhi.shape, fixed), pl.BlockSpec(rlo.shape, fixed)],
        out_specs=[pl.BlockSpec((tm, D_MODEL), row), pl.BlockSpec((tm, TOK_ROW), row),
                   pl.BlockSpec((N_EXPERTS, tm), lambda i: (0, i))],
        out_shape=[jax.ShapeDtypeStruct((n, D_MODEL), F32), jax.ShapeDtypeStruct((n, TOK_ROW), F32),
                   jax.ShapeDtypeStruct((N_EXPERTS, n), F32)],
        compiler_params=_params("parallel"),
        name="mix_out",
    )(a, att, g, x2d, wr, wa, wo, nf, rhi, rlo)


def _topk_kernel(aff_ref, spos_ref, offs_ref, idx_ref, *, cap):
    nchunk = aff_ref.shape[1]
    sl = TOPK_SL
    li = lax.broadcasted_iota(jnp.int32, (LANES, LANES), 0)
    lj = lax.broadcasted_iota(jnp.int32, (LANES, LANES), 1)
    upper = jnp.where(li <= lj, 1.0, 0.0).astype(BF16)
    ci = lax.broadcasted_iota(jnp.int32, (nchunk, nchunk), 0)
    cj = lax.broadcasted_iota(jnp.int32, (nchunk, nchunk), 1)
    lower = jnp.where(cj < ci, 1.0, 0.0).astype(BF16)
    row_ids = lax.broadcasted_iota(jnp.int32, (sl, nchunk), 1).astype(F32)
    slot_iota = lax.broadcasted_iota(jnp.int32, (sl, 1), 0).astype(F32)

    def total(x):
        return jnp.sum(jnp.sum(x, axis=0, keepdims=True), axis=1, keepdims=True)

    def cumsum(mask):
        within = jnp.dot(mask.astype(BF16), upper, preferred_element_type=F32)
        tot = jnp.broadcast_to(within[:, LANES - 1:LANES], within.shape).astype(BF16)
        before = jnp.dot(lower, tot, preferred_element_type=F32)
        return within + before, before

    def per_expert(e, carry):
        bits = pltpu.bitcast(aff_ref[e], jnp.int32)

        def search(i, thr):
            cand = thr | (1 << (30 - i))
            cnt = total(jnp.where(bits >= cand, 1, 0))
            return jnp.where(cnt >= cap, cand, thr)

        thr = lax.fori_loop(0, 31, search, jnp.zeros((1, 1), jnp.int32))
        gt = bits > thr
        eq = bits == thr
        need = (cap - total(jnp.where(gt, 1, 0))).astype(F32)
        eq_f = jnp.where(eq, 1.0, 0.0)
        eq_cum, _ = cumsum(eq_f)
        sel = gt | (eq & (eq_cum - eq_f < need))
        cum, before = cumsum(jnp.where(sel, 1.0, 0.0))
        spos_ref[e] = jnp.where(sel, cum - 1.0, -1.0).astype(jnp.int32)
        offs_ref[e] = before.astype(jnp.int32)
        in_row = (cum - before).astype(BF16)
        ends = jnp.transpose(jnp.broadcast_to(cum[:, LANES - 1:LANES], cum.shape))[0:1, :]

        def slot_tile(jt, carry2):
            j = slot_iota + lax.convert_element_type(jt * sl, F32)
            done = ends <= j
            c_j = jnp.sum(jnp.where(done, 1.0, 0.0), axis=1, keepdims=True)
            before_j = jnp.max(jnp.where(done, ends, 0.0), axis=1, keepdims=True)
            pick = jnp.where(row_ids == c_j, 1.0, 0.0).astype(BF16)
            counts = jnp.dot(pick, in_row, preferred_element_type=F32)
            lane_j = jnp.sum(jnp.where(counts <= j - before_j, 1.0, 0.0), axis=1, keepdims=True)
            tok = c_j * float(LANES) + lane_j
            idx_ref[e, pl.ds(jt, 1), :] = jnp.transpose(jnp.broadcast_to(tok, (sl, LANES)))[0:1, :].astype(jnp.int32)
            return carry2

        lax.fori_loop(0, cap // sl, slot_tile, 0)
        return carry

    lax.fori_loop(0, N_EXPERTS, per_expert, 0)


def _topk(aff3, cap):
    e, nchunk, _ = aff3.shape
    return pl.pallas_call(
        functools.partial(_topk_kernel, cap=cap),
        out_shape=[jax.ShapeDtypeStruct((e, nchunk, LANES), jnp.int32),
                   jax.ShapeDtypeStruct((e, nchunk, LANES), jnp.int32),
                   jax.ShapeDtypeStruct((e, cap // TOPK_SL, TOPK_SL), jnp.int32)],
        compiler_params=pltpu.CompilerParams(vmem_limit_bytes=VMEM_LIMIT),
        name="expert_choice_topk",
    )(aff3)


def _ffn_kernel(idx_ref, tok_hbm, wg_ref, wu_ref, wd_ref, out_ref, xs_ref, sem_ref):
    e = pl.program_id(0)
    s = pl.program_id(1)
    n_steps = pl.num_programs(0) * pl.num_programs(1)
    step = e * pl.num_programs(1) + s
    ts = FFN_TS

    def gather_rows(step_, buf, r0, r1):
        for r in range(r0, r1):
            pltpu.make_async_copy(tok_hbm.at[pl.ds(idx_ref[step_ * ts + r], 1), :],
                                  xs_ref.at[buf, pl.ds(r, 1), :], sem_ref.at[buf]).start()

    def wait_rows(buf):
        pltpu.make_async_copy(tok_hbm.at[pl.ds(0, ts), :], xs_ref.at[buf], sem_ref.at[buf]).wait()

    @pl.when(step == 0)
    def _():
        gather_rows(0, 0, 0, ts)

    def tile(buf):
        wait_rows(buf)
        gather_rows(jnp.minimum(step + 1, n_steps - 1), 1 - buf, 0, ts)
        xs = xs_ref[buf, :, :D_MODEL].astype(BF16)
        aff = xs_ref[buf, :, D_MODEL:]
        lane = lax.broadcasted_iota(jnp.int32, aff.shape, 1)
        route = jnp.sum(jnp.where(lane == e, aff, 0.0), axis=1, keepdims=True)
        gate = jnp.dot(xs, wg_ref[0], preferred_element_type=F32)
        up = jnp.dot(xs, wu_ref[0], preferred_element_type=F32)
        hid = (gate * _sigmoid(gate) * up).astype(BF16)
        out_ref[0] = (jnp.dot(hid, wd_ref[0], preferred_element_type=F32) * route).astype(BF16)

        @pl.when(step == n_steps - 1)
        def _():
            wait_rows(1 - buf)

    for parity in range(2):
        pl.when(step % 2 == parity)(functools.partial(tile, parity))


def _ffn(idx, tok, wg, wu, wd, cap):
    n_tiles = cap // FFN_TS
    wmap = lambda e, s, *_: (e, 0, 0)
    return pl.pallas_call(
        _ffn_kernel,
        grid_spec=pltpu.PrefetchScalarGridSpec(
            num_scalar_prefetch=1,
            grid=(N_EXPERTS, n_tiles),
            in_specs=[pl.BlockSpec(memory_space=pl.ANY),
                      pl.BlockSpec((1, D_MODEL, EXPERT_HIDDEN), wmap),
                      pl.BlockSpec((1, D_MODEL, EXPERT_HIDDEN), wmap),
                      pl.BlockSpec((1, EXPERT_HIDDEN, D_MODEL), wmap)],
            out_specs=pl.BlockSpec((1, FFN_TS, D_MODEL), lambda e, s, *_: (e, s, 0)),
            scratch_shapes=[pltpu.VMEM((2, FFN_TS, TOK_ROW), F32), pltpu.SemaphoreType.DMA((2,))],
        ),
        out_shape=jax.ShapeDtypeStruct((N_EXPERTS, cap, D_MODEL), BF16),
        compiler_params=_params("arbitrary", "arbitrary"),
        name="expert_ffn",
    )(idx, tok, wg, wu, wd)


def _combine_kernel(base_ref, spos_ref, x1_ref, nf_ref, eo_hbm, o_ref, stage_ref, sem_ref, tail_ref, tail_sem,
                    acc_ref, *, cap):
    i = pl.program_id(0)
    tm, win = CMB_TM, CMB_WIN
    row_ids = lax.broadcasted_iota(jnp.int32, (win, tm), 0)
    tn = (((0,), (0,)), ((), ()))

    def window_start(tile, e):
        return jnp.minimum((base_ref[tile * N_EXPERTS + e] // CMB_ALIGN) * CMB_ALIGN, cap - win)

    def window_copy(tile, e, buf):
        a0 = pl.multiple_of(window_start(tile, e), CMB_ALIGN)
        return pltpu.make_async_copy(eo_hbm.at[e, pl.ds(a0, win), :], stage_ref.at[buf, pl.ds(e * win, win), :],
                                     sem_ref.at[buf, e])

    @pl.when(i == 0)
    def _():
        for e in range(N_EXPERTS):
            window_copy(0, e, 0).start()

    @pl.when(i + 1 < pl.num_programs(0))
    def _():
        for e in range(N_EXPERTS):
            window_copy(i + 1, e, (i + 1) % 2).start()

    buf = i % 2
    pieces = []
    for e in range(N_EXPERTS):
        window_copy(i, e, buf).wait()
        pieces.append(jnp.where(spos_ref[e:e + 1, :] == window_start(i, e) + row_ids, 1.0, 0.0).astype(BF16))
    pick = jnp.concatenate(pieces, axis=0)
    acc_ref[...] = x1_ref[...] + lax.dot_general(pick, stage_ref[buf], tn, preferred_element_type=F32)

    for e in range(N_EXPERTS):
        end = base_ref[(i + 1) * N_EXPERTS + e]
        covered = window_start(i, e) + win

        @pl.when(end > covered)
        def _(e=e, end=end, covered=covered):
            def extra(w, carry):
                lo = covered + w * win
                a0 = pl.multiple_of(jnp.minimum(lo, cap - win), CMB_ALIGN)
                cp = pltpu.make_async_copy(eo_hbm.at[e, pl.ds(a0, win), :], tail_ref, tail_sem)
                cp.start()
                cp.wait()
                pos = spos_ref[e:e + 1, :]
                hit = (pos == a0 + row_ids) & (pos >= lo)
                acc_ref[...] += lax.dot_general(jnp.where(hit, 1.0, 0.0).astype(BF16), tail_ref[...], tn,
                                                preferred_element_type=F32)
                return carry

            lax.fori_loop(0, (end - covered + win - 1) // win, extra, 0)

    o_ref[...] = _rms_scale(acc_ref[...]) * nf_ref[...]


def _combine(base, spos, x1, nf, eo):
    n = x1.shape[0]
    cap = eo.shape[1]
    tm = CMB_TM
    row = lambda i, *_: (i, 0)
    return pl.pallas_call(
        functools.partial(_combine_kernel, cap=cap),
        grid_spec=pltpu.PrefetchScalarGridSpec(
            num_scalar_prefetch=1,
            grid=(n // tm,),
            in_specs=[pl.BlockSpec((N_EXPERTS, tm), lambda i, *_: (0, i)),
                      pl.BlockSpec((tm, D_MODEL), row), pl.BlockSpec((1, D_MODEL), lambda i, *_: (0, 0)),
                      pl.BlockSpec(memory_space=pl.ANY)],
            out_specs=pl.BlockSpec((tm, D_MODEL), row),
            scratch_shapes=[pltpu.VMEM((2, N_EXPERTS * CMB_WIN, D_MODEL), BF16),
                            pltpu.SemaphoreType.DMA((2, N_EXPERTS)),
                            pltpu.VMEM((CMB_WIN, D_MODEL), BF16), pltpu.SemaphoreType.DMA,
                            pltpu.VMEM((tm, D_MODEL), F32)],
        ),
        out_shape=jax.ShapeDtypeStruct((n, D_MODEL), F32),
        compiler_params=_params("arbitrary"),
        name="combine_final",
    )(base, spos, x1, nf, eo)


def _moe(x1, h2, aff_t, p):
    n = x1.shape[0]
    cap = EC_CAPACITY_FACTOR * n // N_EXPERTS
    spos, offs, idx = _topk(aff_t.reshape(N_EXPERTS, n // LANES, LANES), cap)
    eo = _ffn(idx.reshape(-1), h2, p["w_exp_gate"], p["w_exp_up"], p["w_exp_down"], cap)
    base = offs[:, ::CMB_TM // LANES, 0].T
    base = jnp.concatenate([base, jnp.full((1, N_EXPERTS), cap, jnp.int32)], axis=0).reshape(-1)
    return _combine(base, spos.reshape(N_EXPERTS, n), x1, p["norm_final"], eo)


def _pair_perm():
    base = np.concatenate([np.arange(0, 32), np.arange(64, 96), np.arange(32, 64), np.arange(96, 128)])
    return np.concatenate([base + LANES * b for b in range(AQ // LANES)])


def _rope_tables():
    pos = jnp.arange(SEQ, dtype=F32)
    ret_inv = 1.0 / (ROPE_THETA ** jnp.linspace(0.0, 1.0, RET_DK // 2, dtype=F32))
    ang = pos[:, None] * ret_inv[None, :]
    ret_cos = jnp.concatenate([jnp.cos(ang), jnp.cos(ang)], axis=-1)
    ret_sin = jnp.concatenate([-jnp.sin(ang), jnp.sin(ang)], axis=-1)
    att_inv = ROPE_THETA ** (-jnp.arange(0, ATT_DH, 2, dtype=F32) / ATT_DH)
    ang = pos[:, None] * att_inv[None, :]
    c, s = jnp.cos(ang), jnp.sin(ang)
    att_cos = jnp.concatenate([c, c, c, c], axis=-1)
    att_sin = jnp.concatenate([-s, -s, s, s], axis=-1)
    att_tabs = []
    for dil in ATT_DILATIONS:
        for tab in (att_cos, att_sin):
            att_tabs.append(tab.reshape(SEQ // dil, dil, LANES).transpose(1, 0, 2))
    return ret_cos, ret_sin, att_tabs


def _encoder(x, p):
    batch = x.shape[0]
    x2d = x.reshape(batch * SEQ, D_MODEL)
    rq, rk, rv, rg, gates = _proj_ret(x2d, p["norm_mix"], p["w_ret"], p["ret_cos"], p["ret_sin"])
    qkv = _proj_att(x2d, p["norm_mix"], p["w_att"], p["att_tabs"], batch)
    a = _retention(p["lgf"], p["lgb"], rq, rk, rv, rg, p["ret_norm"], batch)
    att = _attention(qkv, batch)
    x1, h2, aff_t = _mix_out(a, att, gates, x2d, p["w_ret_out"], p["w_att_out"], p["w_out"], p["norm_ffn"],
                             p["router_hi"], p["router_lo"])
    return _moe(x1, h2, aff_t, p).reshape(batch, SEQ, D_MODEL)


def kernel(x_prompt, x_sample, norm_mix, w_in, ret_log_decay_fwd, ret_log_decay_bwd, ret_norm, w_ret_out,
           w_att_out, w_out, norm_ffn, w_router, w_exp_gate, w_exp_up, w_exp_down, norm_final):
    w = w_in[0]
    c_rv = 2 * RQ
    c_rg = c_rv + RV
    c_aq = c_rg + RV
    c_ak = c_aq + AQ
    c_av = c_ak + AQ
    c_gate = c_av + AQ
    perm = _pair_perm()
    w_aq = w[:, c_aq:c_ak][:, perm] * (ATT_DH ** -0.5)
    w_ak = w[:, c_ak:c_av][:, perm]
    w_av = w[:, c_av:c_gate]
    ret_cos, ret_sin, att_tabs = _rope_tables()
    router_t = w_router[0].T
    router_hi = router_t.astype(BF16)
    p = {
        "norm_mix": norm_mix,
        "w_ret": jnp.concatenate([w[:, :c_aq], w[:, c_gate:]], axis=1).astype(BF16),
        "w_att": jnp.concatenate([w_aq, w_ak, w_av], axis=1).astype(BF16),
        "ret_cos": ret_cos, "ret_sin": ret_sin, "att_tabs": att_tabs,
        "lgf": ret_log_decay_fwd[0], "lgb": ret_log_decay_bwd[0],
        "ret_norm": ret_norm,
        "w_ret_out": w_ret_out[0].astype(BF16), "w_att_out": w_att_out[0].astype(BF16),
        "w_out": w_out[0].astype(BF16),
        "norm_ffn": norm_ffn,
        "router_hi": router_hi, "router_lo": (router_t - router_hi.astype(F32)).astype(BF16),
        "w_exp_gate": w_exp_gate[0].astype(BF16), "w_exp_up": w_exp_up[0].astype(BF16),
        "w_exp_down": w_exp_down[0].astype(BF16),
        "norm_final": norm_final.reshape(1, D_MODEL),
    }
    return _encoder(x_prompt, p), _encoder(x_sample, p)
```

```python
import functools

import numpy as np
import jax
import jax.numpy as jnp
from jax import lax
from jax.experimental import pallas as pl
from jax.experimental.pallas import tpu as pltpu

F32 = jnp.float32
BF16 = jnp.bfloat16

D_MODEL = 1024
SEQ = 4096
RET_HEADS = 4
RET_DK = 128
RET_DV = 256
RET_CHUNK = 256
RET_UNROLL = 16
RET_FIN = 512
ATT_DILATIONS = (1, 4, 16)
ATT_HALF_SPAN = 64
ATT_HEADS_PER_GROUP = 8
ATT_DH = 64
ATT_PAIRS = ATT_HEADS_PER_GROUP // 2
N_EXPERTS = 16
EC_CAPACITY_FACTOR = 2
EXPERT_HIDDEN = 2048
ROPE_THETA = 10000.0
EPS = 1e-6
RQ = RET_HEADS * RET_DK
RV = RET_HEADS * RET_DV
AQ = 3 * ATT_HEADS_PER_GROUP * ATT_DH
A_OUT = ATT_HEADS_PER_GROUP * ATT_DH
LANES = 128

PROJ_TM = 512
ATT_QB = 128
ATT_KW = ATT_QB + 2 * ATT_HALF_SPAN
ATT_UNROLL = 32
FFN_TS = 256
TOK_ROW = D_MODEL + LANES
TOPK_SL = 512
TOPK_GROUP = 2
TOPK_TILES = 2
CMB_TM = 512
CMB_WIN = 112
CMB_ALIGN = 16
MASK_NEG = -1e30
VMEM_LIMIT = 56 * 1024 * 1024


def _params(*sem):
    return pltpu.CompilerParams(dimension_semantics=sem, vmem_limit_bytes=VMEM_LIMIT)


def _sigmoid(x):
    return 1.0 / (1.0 + jnp.exp(-x))


def _rms_scale(x):
    return x * lax.rsqrt(jnp.mean(x * x, axis=-1, keepdims=True) + EPS)


def _rope(a, cos, sin):
    return a * cos + pltpu.roll(a, 64, 1) * sin


def _proj_ret_kernel(x_ref, nm_ref, w_ref, cos_ref, sin_ref, rq_ref, rk_ref, rv_ref, rg_ref, g_ref):
    h = (_rms_scale(x_ref[...]) * nm_ref[...]).astype(BF16)
    cos = cos_ref[...]
    sin = sin_ref[...]

    def proj(c0, width):
        return jnp.dot(h, w_ref[:, c0:c0 + width], preferred_element_type=F32)

    acc = proj(0, RQ)
    for hh in range(RET_HEADS):
        sl = slice(hh * LANES, (hh + 1) * LANES)
        rq_ref[:, sl] = _rope(acc[:, sl], cos, sin).astype(BF16)
    acc = proj(RQ, RQ)
    for hh in range(RET_HEADS):
        sl = slice(hh * LANES, (hh + 1) * LANES)
        rk_ref[:, sl] = (_rope(acc[:, sl], cos, sin) * (RET_DK ** -0.5)).astype(BF16)
    for c in range(RV // 512):
        rv_ref[:, c * 512:(c + 1) * 512] = proj(2 * RQ + c * 512, 512).astype(BF16)
    for c in range(RV // 512):
        acc = proj(2 * RQ + RV + c * 512, 512)
        rg_ref[:, c * 512:(c + 1) * 512] = (acc * _sigmoid(acc)).astype(BF16)
    for c in range(2 * D_MODEL // 512):
        acc = proj(2 * RQ + 2 * RV + c * 512, 512)
        g_ref[:, c * 512:(c + 1) * 512] = _sigmoid(acc).astype(BF16)


def _proj_ret(x2d, nm, w, cos, sin):
    n = x2d.shape[0]
    nt = SEQ // PROJ_TM
    tm = PROJ_TM
    row = lambda i: (i, 0)
    fixed = lambda i: (0, 0)
    tab = lambda i: (i % nt, 0)
    return pl.pallas_call(
        _proj_ret_kernel,
        grid=(n // tm,),
        in_specs=[pl.BlockSpec((tm, D_MODEL), row), pl.BlockSpec((1, D_MODEL), fixed),
                  pl.BlockSpec(w.shape, fixed), pl.BlockSpec((tm, LANES), tab), pl.BlockSpec((tm, LANES), tab)],
        out_specs=[pl.BlockSpec((tm, RQ), row), pl.BlockSpec((tm, RQ), row), pl.BlockSpec((tm, RV), row),
                   pl.BlockSpec((tm, RV), row), pl.BlockSpec((tm, 2 * D_MODEL), row)],
        out_shape=[jax.ShapeDtypeStruct((n, RQ), BF16), jax.ShapeDtypeStruct((n, RQ), BF16),
                   jax.ShapeDtypeStruct((n, RV), BF16), jax.ShapeDtypeStruct((n, RV), BF16),
                   jax.ShapeDtypeStruct((n, 2 * D_MODEL), BF16)],
        compiler_params=_params("parallel"),
        name="proj_ret",
    )(x2d, nm, w, cos, sin)


def _proj_att_kernel(x_ref, nm_ref, w_ref, c0_ref, s0_ref, c1_ref, s1_ref, c2_ref, s2_ref, *rest):
    out_refs = rest[:9]
    scr_ref = rest[9]
    h = (_rms_scale(x_ref[...]) * nm_ref[...]).astype(BF16)
    tabs = ((c0_ref, s0_ref), (c1_ref, s1_ref), (c2_ref, s2_ref))
    tm = x_ref.shape[0]
    for sec in range(3):
        for g, dil in enumerate(ATT_DILATIONS):
            c0 = (sec * 3 + g) * A_OUT
            acc = jnp.dot(h, w_ref[:, c0:c0 + A_OUT], preferred_element_type=F32)
            out_ref = out_refs[sec * 3 + g]
            if dil > 1:
                for p in range(ATT_PAIRS):
                    scr_ref[p] = acc[:, p * LANES:(p + 1) * LANES]
            for r in range(dil):
                for p in range(ATT_PAIRS):
                    if dil == 1:
                        a = acc[:, p * LANES:(p + 1) * LANES]
                    else:
                        a = scr_ref[p, pl.ds(r, tm // dil, stride=dil), :]
                    if sec < 2:
                        a = _rope(a, tabs[g][0][r], tabs[g][1][r])
                    out_ref[0, p, r] = a.astype(BF16)


def _proj_att(x2d, nm, w, tabs, batch):
    n = x2d.shape[0]
    tm = PROJ_TM
    nt = SEQ // tm
    row = lambda i: (i, 0)
    fixed = lambda i: (0, 0)
    in_specs = [pl.BlockSpec((tm, D_MODEL), row), pl.BlockSpec((1, D_MODEL), fixed), pl.BlockSpec(w.shape, fixed)]
    for dil in ATT_DILATIONS:
        for _ in range(2):
            in_specs.append(pl.BlockSpec((dil, tm // dil, LANES), lambda i: (0, i % nt, 0)))
    out_specs, out_shape = [], []
    for _ in range(3):
        for dil in ATT_DILATIONS:
            out_specs.append(pl.BlockSpec((1, ATT_PAIRS, dil, tm // dil, LANES),
                                          lambda i: (i // nt, 0, 0, i % nt, 0)))
            out_shape.append(jax.ShapeDtypeStruct((batch, ATT_PAIRS, dil, SEQ // dil, LANES), BF16))
    return pl.pallas_call(
        _proj_att_kernel,
        grid=(n // tm,),
        in_specs=in_specs,
        out_specs=out_specs,
        out_shape=out_shape,
        scratch_shapes=[pltpu.VMEM((ATT_PAIRS, tm, LANES), F32)],
        compiler_params=_params("parallel"),
        name="proj_att",
    )(x2d, nm, w, *tabs)


def _retention_kernel(lgf_ref, lgb_ref, q_ref, k_ref, v_ref, rg_ref, nrm_ref, o_ref, part_ref, cross_ref):
    hd = pl.program_id(1)
    lgf = lgf_ref[hd]
    lgb = lgb_ref[hd]
    c = RET_CHUNK
    nc = q_ref.shape[0] // c
    ti = lax.broadcasted_iota(jnp.int32, (c, c), 0)
    si = lax.broadcasted_iota(jnp.int32, (c, c), 1)
    diff = (ti - si).astype(F32)
    fwd = diff >= 0.0
    decay = jnp.where(fwd, jnp.exp(lgf * jnp.where(fwd, diff, 0.0)), jnp.exp(lgb * jnp.where(fwd, 0.0, -diff)))
    idx = lax.broadcasted_iota(jnp.int32, (c, 1), 0).astype(F32)
    qf_dec = jnp.exp(lgf * (idx + 1.0))
    kf_dec = jnp.exp(lgf * (c - 1.0 - idx))
    qb_dec = jnp.exp(lgb * (c - idx))
    kb_dec = jnp.exp(lgb * idx)
    one = jnp.ones((1, 1), F32)
    cdf = jnp.exp(one * (lgf * c))
    cdb = jnp.exp(one * (lgb * c))
    nrm = nrm_ref[...]
    tn = (((0,), (0,)), ((), ()))
    nt = (((1,), (1,)), ((), ()))

    def step(j, states):
        for u in range(RET_UNROLL):
            states = one_chunk(j * RET_UNROLL + u, states)
        return states

    def one_chunk(i, states):
        sf, sb = states
        rows = pl.ds(pl.multiple_of(i * c, c), c)
        q = q_ref[rows, :]
        k = k_ref[rows, :]
        v = v_ref[rows, :]
        scores = lax.dot_general(q, k, nt, preferred_element_type=F32) * decay
        part = jnp.dot(scores.astype(BF16), v, preferred_element_type=F32)
        part += jnp.dot((q.astype(F32) * qf_dec).astype(BF16), sf.astype(BF16), preferred_element_type=F32)
        part_ref[rows, :] = part
        sf = sf * cdf + lax.dot_general((k.astype(F32) * kf_dec).astype(BF16), v, tn, preferred_element_type=F32)

        rows = pl.ds(pl.multiple_of((nc - 1 - i) * c, c), c)
        q = q_ref[rows, :].astype(F32)
        k = k_ref[rows, :].astype(F32)
        cross_ref[rows, :] = jnp.dot((q * qb_dec).astype(BF16), sb.astype(BF16), preferred_element_type=F32)
        sb = sb * cdb + lax.dot_general((k * kb_dec).astype(BF16), v_ref[rows, :], tn, preferred_element_type=F32)
        return sf, sb

    zero = jnp.zeros((RET_DK, RET_DV), F32)
    lax.fori_loop(0, nc // RET_UNROLL, step, (zero, zero))

    def finish(i, carry):
        rows = pl.ds(pl.multiple_of(i * RET_FIN, RET_FIN), RET_FIN)
        out = part_ref[rows, :] + cross_ref[rows, :]
        o_ref[rows, :] = (rg_ref[rows, :].astype(F32) * (_rms_scale(out) * nrm)).astype(BF16)
        return carry

    lax.fori_loop(0, q_ref.shape[0] // RET_FIN, finish, 0)


def _retention(lgf, lgb, rq, rk, rv, rg, nrm, batch):
    n = rq.shape[0]
    bh = lambda b, h, *_: (b, h)
    return pl.pallas_call(
        _retention_kernel,
        grid_spec=pltpu.PrefetchScalarGridSpec(
            num_scalar_prefetch=2,
            grid=(batch, RET_HEADS),
            in_specs=[pl.BlockSpec((SEQ, RET_DK), bh), pl.BlockSpec((SEQ, RET_DK), bh),
                      pl.BlockSpec((SEQ, RET_DV), bh), pl.BlockSpec((SEQ, RET_DV), bh),
                      pl.BlockSpec((1, RET_DV), lambda b, h, *_: (0, h))],
            out_specs=pl.BlockSpec((SEQ, RET_DV), bh),
            scratch_shapes=[pltpu.VMEM((SEQ, RET_DV), F32), pltpu.VMEM((SEQ, RET_DV), F32)],
        ),
        out_shape=jax.ShapeDtypeStruct((n, RV), BF16),
        compiler_params=_params("parallel", "parallel"),
        name="retention",
    )(lgf, lgb, rq, rk, rv, rg, nrm)


def _attn_block(q, kw, vw, bias):
    lane = lax.broadcasted_iota(jnp.int32, q.shape, 1)
    head0_qk = (lane < 32) | ((lane >= 64) & (lane < 96))
    outs, maxs, sums = [], [], []
    for hh in range(2):
        qm = jnp.where(head0_qk if hh == 0 else jnp.logical_not(head0_qk), q, jnp.zeros_like(q))
        s = lax.dot_general(qm, kw, (((1,), (1,)), ((), ())), preferred_element_type=F32) + bias
        m = jnp.max(s, axis=-1, keepdims=True)
        p = jnp.exp(s - m)
        outs.append(jnp.dot(p.astype(BF16), vw, preferred_element_type=F32))
        maxs.append(m)
        sums.append(jnp.sum(p, axis=-1, keepdims=True))
    head0_v = lane < ATT_DH
    return (jnp.where(head0_v, outs[0], outs[1]), jnp.where(head0_v, maxs[0], maxs[1]),
            jnp.where(head0_v, sums[0], sums[1]))


def _attention_kernel(q0, k0, v0, q1, k1, v1, q2, k2, v2, o_ref, o1_ref, m1_ref, l1_ref, o2_ref, m2_ref, l2_ref,
                      bias_ref):
    qb, kw = ATT_QB, ATT_KW
    ji = lax.broadcasted_iota(jnp.int32, (qb, kw), 1) - lax.broadcasted_iota(jnp.int32, (qb, kw), 0)
    for n in range(3):
        rel = ji - n * ATT_HALF_SPAN
        bias_ref[n] = jnp.where((rel >= -ATT_HALF_SPAN) & (rel <= ATT_HALF_SPAN), 0.0, MASK_NEG).astype(F32)

    def group(q_ref, k_ref, v_ref, dil, finish):
        sub_len = SEQ // dil
        nblk = sub_len // qb

        def body(jt, carry):
            for u in range(ATT_UNROLL):
                it = jt * ATT_UNROLL + u
                r = it // nblk
                l0 = (it % nblk) * qb
                kstart = pl.multiple_of(jnp.clip(l0 - ATT_HALF_SPAN, 0, sub_len - kw), ATT_HALF_SPAN)
                bias = bias_ref[(l0 - kstart) // ATT_HALF_SPAN]
                q = q_ref[0, 0, r, pl.ds(pl.multiple_of(l0, qb), qb), :]
                kwin = k_ref[0, 0, r, pl.ds(kstart, kw), :]
                vwin = v_ref[0, 0, r, pl.ds(kstart, kw), :]
                finish(l0, r, *_attn_block(q, kwin, vwin, bias))
            return carry

        lax.fori_loop(0, dil * nblk // ATT_UNROLL, body, 0)

    def store_strided(o_nat, m_nat, l_nat, dil):
        def finish(l0, r, o, m, l):
            rows = pl.ds(l0 * dil + r, ATT_QB, stride=dil)
            o_nat[rows, :] = o
            m_nat[rows, :] = m
            l_nat[rows, :] = l
        return finish

    group(q2, k2, v2, ATT_DILATIONS[2], store_strided(o2_ref, m2_ref, l2_ref, ATT_DILATIONS[2]))
    group(q1, k1, v1, ATT_DILATIONS[1], store_strided(o1_ref, m1_ref, l1_ref, ATT_DILATIONS[1]))

    def merge(l0, r, o, m, l):
        rows = pl.ds(pl.multiple_of(l0, ATT_QB), ATT_QB)
        ma, mb = m1_ref[rows, :], m2_ref[rows, :]
        mx = jnp.maximum(m, jnp.maximum(ma, mb))
        w0, w1, w2 = jnp.exp(m - mx), jnp.exp(ma - mx), jnp.exp(mb - mx)
        num = w0 * o + w1 * o1_ref[rows, :] + w2 * o2_ref[rows, :]
        den = w0 * l + w1 * l1_ref[rows, :] + w2 * l2_ref[rows, :]
        o_ref[rows, :] = (num / den).astype(BF16)

    group(q0, k0, v0, ATT_DILATIONS[0], merge)


def _attention(qkv, batch):
    in_specs = []
    for g, dil in enumerate(ATT_DILATIONS):
        for _ in range(3):
            in_specs.append(pl.BlockSpec((1, 1, dil, SEQ // dil, LANES), lambda b, p: (b, p, 0, 0, 0)))
    args = []
    for g in range(3):
        args += [qkv[0 * 3 + g], qkv[1 * 3 + g], qkv[2 * 3 + g]]
    return pl.pallas_call(
        _attention_kernel,
        grid=(batch, ATT_PAIRS),
        in_specs=in_specs,
        out_specs=pl.BlockSpec((SEQ, LANES), lambda b, p: (b, p)),
        out_shape=jax.ShapeDtypeStruct((batch * SEQ, A_OUT), BF16),
        scratch_shapes=[pltpu.VMEM((SEQ, LANES), F32) for _ in range(6)]
        + [pltpu.VMEM((3, ATT_QB, ATT_KW), F32)],
        compiler_params=_params("parallel", "parallel"),
        name="dilated_attention",
    )(*args)


def _mix_out_kernel(a_ref, att_ref, g_ref, x_ref, wr_ref, wa_ref, wo_ref, nf_ref, rhi_ref, rlo_ref,
                    x1_ref, h2_ref, aff_ref):
    bra = jnp.dot(a_ref[...], wr_ref[...], preferred_element_type=F32)
    brb = jnp.dot(att_ref[...], wa_ref[...], preferred_element_type=F32)
    g = g_ref[...].astype(F32)
    merged = g[:, :D_MODEL] * bra + g[:, D_MODEL:] * brb
    x1 = x_ref[...] + jnp.dot(merged.astype(BF16), wo_ref[...], preferred_element_type=F32)
    x1_ref[...] = x1
    h2 = _rms_scale(x1) * nf_ref[...]
    hi = h2.astype(BF16)
    lo = (h2 - hi.astype(F32)).astype(BF16)
    nt = (((1,), (1,)), ((), ()))
    logits = (lax.dot_general(rhi_ref[...], hi, nt, preferred_element_type=F32)
              + lax.dot_general(rhi_ref[...], lo, nt, preferred_element_type=F32)
              + lax.dot_general(rlo_ref[...], hi, nt, preferred_element_type=F32))
    e = jnp.exp(logits - jnp.max(logits, axis=0, keepdims=True))
    aff = e / jnp.sum(e, axis=0, keepdims=True)
    aff_ref[...] = aff
    h2_ref[:, :D_MODEL] = h2
    pad = jnp.zeros((LANES - N_EXPERTS, aff.shape[1]), F32)
    h2_ref[:, D_MODEL:] = jnp.transpose(jnp.concatenate([aff, pad], axis=0))


def _mix_out(a, att, g, x2d, wr, wa, wo, nf, rhi, rlo):
    n = x2d.shape[0]
    tm = PROJ_TM
    row = lambda i: (i, 0)
    fixed = lambda i: (0, 0)
    return pl.pallas_call(
        _mix_out_kernel,
        grid=(n // tm,),
        in_specs=[pl.BlockSpec((tm, RV), row), pl.BlockSpec((tm, A_OUT), row), pl.BlockSpec((tm, 2 * D_MODEL), row),
                  pl.BlockSpec((tm, D_MODEL), row), pl.BlockSpec(wr.shape, fixed), pl.BlockSpec(wa.shape, fixed),
                  pl.BlockSpec(wo.shape, fixed), pl.BlockSpec((1, D_MODEL), fixed),
                  pl.BlockSpec(rhi.shape, fixed), pl.BlockSpec(rlo.shape, fixed)],
        out_specs=[pl.BlockSpec((tm, D_MODEL), row), pl.BlockSpec((tm, TOK_ROW), row),
                   pl.BlockSpec((N_EXPERTS, tm), lambda i: (0, i))],
        out_shape=[jax.ShapeDtypeStruct((n, D_MODEL), F32), jax.ShapeDtypeStruct((n, TOK_ROW), F32),
                   jax.ShapeDtypeStruct((N_EXPERTS, n), F32)],
        compiler_params=_params("parallel"),
        name="mix_out",
    )(a, att, g, x2d, wr, wa, wo, nf, rhi, rlo)


def _topk_kernel(aff_ref, spos_ref, offs_ref, idx_ref, *, cap):
    nchunk = aff_ref.shape[1]
    sl = TOPK_SL
    li = lax.broadcasted_iota(jnp.int32, (LANES, LANES), 0)
    lj = lax.broadcasted_iota(jnp.int32, (LANES, LANES), 1)
    upper = jnp.where(li <= lj, 1.0, 0.0).astype(BF16)
    ci = lax.broadcasted_iota(jnp.int32, (nchunk, nchunk), 0)
    cj = lax.broadcasted_iota(jnp.int32, (nchunk, nchunk), 1)
    lower = jnp.where(cj < ci, 1.0, 0.0).astype(BF16)
    row_ids = lax.broadcasted_iota(jnp.int32, (sl, nchunk), 1).astype(F32)
    slot_iota = lax.broadcasted_iota(jnp.int32, (sl, 1), 0).astype(F32)

    def total(x):
        return jnp.sum(jnp.sum(x, axis=0, keepdims=True), axis=1, keepdims=True)

    def cumsum(mask):
        within = jnp.dot(mask.astype(BF16), upper, preferred_element_type=F32)
        tot = jnp.broadcast_to(within[:, LANES - 1:LANES], within.shape).astype(BF16)
        before = jnp.dot(lower, tot, preferred_element_type=F32)
        return within + before, before

    def expert_group(g, carry):
        experts = [g * TOPK_GROUP + u for u in range(TOPK_GROUP)]
        all_bits = [pltpu.bitcast(aff_ref[e], jnp.int32) for e in experts]

        def search(i, thrs):
            out = []
            for bits, thr in zip(all_bits, thrs):
                cand = thr | (1 << (30 - i))
                cnt = total(jnp.where(bits >= cand, 1, 0))
                out.append(jnp.where(cnt >= cap, cand, thr))
            return tuple(out)

        thrs = lax.fori_loop(0, 31, search, tuple(jnp.zeros((1, 1), jnp.int32) for _ in experts))
        inverters = [select(e, bits, thr) for e, bits, thr in zip(experts, all_bits, thrs)]

        def slot_tiles(jj, carry2):
            for u in range(TOPK_TILES):
                for invert in inverters:
                    invert(jj * TOPK_TILES + u)
            return carry2

        lax.fori_loop(0, cap // sl // TOPK_TILES, slot_tiles, 0)
        return carry

    def select(e, bits, thr):
        gt = bits > thr
        eq = bits == thr
        need = (cap - total(jnp.where(gt, 1, 0))).astype(F32)
        eq_f = jnp.where(eq, 1.0, 0.0)
        eq_cum, _ = cumsum(eq_f)
        sel = gt | (eq & (eq_cum - eq_f < need))
        cum, before = cumsum(jnp.where(sel, 1.0, 0.0))
        spos_ref[e] = jnp.where(sel, cum - 1.0, -1.0).astype(jnp.int32)
        offs_ref[e] = before.astype(jnp.int32)
        in_row = (cum - before).astype(BF16)
        ends = jnp.transpose(jnp.broadcast_to(cum[:, LANES - 1:LANES], cum.shape))[0:1, :]

        def slot_tile(jt):
            j = slot_iota + lax.convert_element_type(jt * sl, F32)
            done = ends <= j
            c_j = jnp.sum(jnp.where(done, 1.0, 0.0), axis=1, keepdims=True)
            before_j = jnp.max(jnp.where(done, ends, 0.0), axis=1, keepdims=True)
            pick = jnp.where(row_ids == c_j, 1.0, 0.0).astype(BF16)
            counts = jnp.dot(pick, in_row, preferred_element_type=F32)
            lane_j = jnp.sum(jnp.where(counts <= j - before_j, 1.0, 0.0), axis=1, keepdims=True)
            tok = c_j * float(LANES) + lane_j
            idx_ref[e, pl.ds(jt, 1), :] = jnp.transpose(jnp.broadcast_to(tok, (sl, LANES)))[0:1, :].astype(jnp.int32)

        return slot_tile

    lax.fori_loop(0, N_EXPERTS // TOPK_GROUP, expert_group, 0)


def _topk(aff3, cap):
    e, nchunk, _ = aff3.shape
    assert e % TOPK_GROUP == 0 and cap % (TOPK_SL * TOPK_TILES) == 0
    return pl.pallas_call(
        functools.partial(_topk_kernel, cap=cap),
        out_shape=[jax.ShapeDtypeStruct((e, nchunk, LANES), jnp.int32),
                   jax.ShapeDtypeStruct((e, nchunk, LANES), jnp.int32),
                   jax.ShapeDtypeStruct((e, cap // TOPK_SL, TOPK_SL), jnp.int32)],
        compiler_params=pltpu.CompilerParams(vmem_limit_bytes=VMEM_LIMIT),
        name="expert_choice_topk",
    )(aff3)


def _ffn_kernel(idx_ref, tok_hbm, wg_ref, wu_ref, wd_ref, out_ref, xs_ref, sem_ref):
    e = pl.program_id(0)
    s = pl.program_id(1)
    n_steps = pl.num_programs(0) * pl.num_programs(1)
    step = e * pl.num_programs(1) + s
    ts = FFN_TS

    def gather_rows(step_, buf, r0, r1):
        for r in range(r0, r1):
            pltpu.make_async_copy(tok_hbm.at[pl.ds(idx_ref[step_ * ts + r], 1), :],
                                  xs_ref.at[buf, pl.ds(r, 1), :], sem_ref.at[buf]).start()

    def wait_rows(buf):
        pltpu.make_async_copy(tok_hbm.at[pl.ds(0, ts), :], xs_ref.at[buf], sem_ref.at[buf]).wait()

    @pl.when(step == 0)
    def _():
        gather_rows(0, 0, 0, ts)

    def tile(buf):
        wait_rows(buf)
        gather_rows(jnp.minimum(step + 1, n_steps - 1), 1 - buf, 0, ts)
        xs = xs_ref[buf, :, :D_MODEL].astype(BF16)
        aff = xs_ref[buf, :, D_MODEL:]
        lane = lax.broadcasted_iota(jnp.int32, aff.shape, 1)
        route = jnp.sum(jnp.where(lane == e, aff, 0.0), axis=1, keepdims=True)
        gate = jnp.dot(xs, wg_ref[0], preferred_element_type=F32)
        up = jnp.dot(xs, wu_ref[0], preferred_element_type=F32)
        hid = (gate * _sigmoid(gate) * up).astype(BF16)
        out_ref[0] = (jnp.dot(hid, wd_ref[0], preferred_element_type=F32) * route).astype(BF16)

        @pl.when(step == n_steps - 1)
        def _():
            wait_rows(1 - buf)

    for parity in range(2):
        pl.when(step % 2 == parity)(functools.partial(tile, parity))


def _ffn(idx, tok, wg, wu, wd, cap):
    n_tiles = cap // FFN_TS
    wmap = lambda e, s, *_: (e, 0, 0)
    return pl.pallas_call(
        _ffn_kernel,
        grid_spec=pltpu.PrefetchScalarGridSpec(
            num_scalar_prefetch=1,
            grid=(N_EXPERTS, n_tiles),
            in_specs=[pl.BlockSpec(memory_space=pl.ANY),
                      pl.BlockSpec((1, D_MODEL, EXPERT_HIDDEN), wmap),
                      pl.BlockSpec((1, D_MODEL, EXPERT_HIDDEN), wmap),
                      pl.BlockSpec((1, EXPERT_HIDDEN, D_MODEL), wmap)],
            out_specs=pl.BlockSpec((1, FFN_TS, D_MODEL), lambda e, s, *_: (e, s, 0)),
            scratch_shapes=[pltpu.VMEM((2, FFN_TS, TOK_ROW), F32), pltpu.SemaphoreType.DMA((2,))],
        ),
        out_shape=jax.ShapeDtypeStruct((N_EXPERTS, cap, D_MODEL), BF16),
        compiler_params=_params("arbitrary", "arbitrary"),
        name="expert_ffn",
    )(idx, tok, wg, wu, wd)


def _combine_kernel(base_ref, spos_ref, x1_ref, nf_ref, eo_hbm, o_ref, stage_ref, sem_ref, tail_ref, tail_sem,
                    acc_ref, *, cap):
    i = pl.program_id(0)
    tm, win = CMB_TM, CMB_WIN
    row_ids = lax.broadcasted_iota(jnp.int32, (win, tm), 0)
    tn = (((0,), (0,)), ((), ()))

    def window_start(tile, e):
        return jnp.minimum((base_ref[tile * N_EXPERTS + e] // CMB_ALIGN) * CMB_ALIGN, cap - win)

    def window_copy(tile, e, buf):
        a0 = pl.multiple_of(window_start(tile, e), CMB_ALIGN)
        return pltpu.make_async_copy(eo_hbm.at[e, pl.ds(a0, win), :], stage_ref.at[buf, pl.ds(e * win, win), :],
                                     sem_ref.at[buf, e])

    @pl.when(i == 0)
    def _():
        for e in range(N_EXPERTS):
            window_copy(0, e, 0).start()

    @pl.when(i + 1 < pl.num_programs(0))
    def _():
        for e in range(N_EXPERTS):
            window_copy(i + 1, e, (i + 1) % 2).start()

    buf = i % 2
    pieces = []
    for e in range(N_EXPERTS):
        window_copy(i, e, buf).wait()
        pieces.append(jnp.where(spos_ref[e:e + 1, :] == window_start(i, e) + row_ids, 1.0, 0.0).astype(BF16))
    pick = jnp.concatenate(pieces, axis=0)
    acc_ref[...] = x1_ref[...] + lax.dot_general(pick, stage_ref[buf], tn, preferred_element_type=F32)

    for e in range(N_EXPERTS):
        end = base_ref[(i + 1) * N_EXPERTS + e]
        covered = window_start(i, e) + win

        @pl.when(end > covered)
        def _(e=e, end=end, covered=covered):
            def extra(w, carry):
                lo = covered + w * win
                a0 = pl.multiple_of(jnp.minimum(lo, cap - win), CMB_ALIGN)
                cp = pltpu.make_async_copy(eo_hbm.at[e, pl.ds(a0, win), :], tail_ref, tail_sem)
                cp.start()
                cp.wait()
                pos = spos_ref[e:e + 1, :]
                hit = (pos == a0 + row_ids) & (pos >= lo)
                acc_ref[...] += lax.dot_general(jnp.where(hit, 1.0, 0.0).astype(BF16), tail_ref[...], tn,
                                                preferred_element_type=F32)
                return carry

            lax.fori_loop(0, (end - covered + win - 1) // win, extra, 0)

    o_ref[...] = _rms_scale(acc_ref[...]) * nf_ref[...]


def _combine(base, spos, x1, nf, eo):
    n = x1.shape[0]
    cap = eo.shape[1]
    tm = CMB_TM
    row = lambda i, *_: (i, 0)
    return pl.pallas_call(
        functools.partial(_combine_kernel, cap=cap),
        grid_spec=pltpu.PrefetchScalarGridSpec(
            num_scalar_prefetch=1,
            grid=(n // tm,),
            in_specs=[pl.BlockSpec((N_EXPERTS, tm), lambda i, *_: (0, i)),
                      pl.BlockSpec((tm, D_MODEL), row), pl.BlockSpec((1, D_MODEL), lambda i, *_: (0, 0)),
                      pl.BlockSpec(memory_space=pl.ANY)],
            out_specs=pl.BlockSpec((tm, D_MODEL), row),
            scratch_shapes=[pltpu.VMEM((2, N_EXPERTS * CMB_WIN, D_MODEL), BF16),
                            pltpu.SemaphoreType.DMA((2, N_EXPERTS)),
                            pltpu.VMEM((CMB_WIN, D_MODEL), BF16), pltpu.SemaphoreType.DMA,
                            pltpu.VMEM((tm, D_MODEL), F32)],
        ),
        out_shape=jax.ShapeDtypeStruct((n, D_MODEL), F32),
        compiler_params=_params("arbitrary"),
        name="combine_final",
    )(base, spos, x1, nf, eo)


def _moe(x1, h2, aff_t, p):
    n = x1.shape[0]
    cap = EC_CAPACITY_FACTOR * n // N_EXPERTS
    spos, offs, idx = _topk(aff_t.reshape(N_EXPERTS, n // LANES, LANES), cap)
    eo = _ffn(idx.reshape(-1), h2, p["w_exp_gate"], p["w_exp_up"], p["w_exp_down"], cap)
    base = offs[:, ::CMB_TM // LANES, 0].T
    base = jnp.concatenate([base, jnp.full((1, N_EXPERTS), cap, jnp.int32)], axis=0).reshape(-1)
    return _combine(base, spos.reshape(N_EXPERTS, n), x1, p["norm_final"], eo)


def _pair_perm():
    base = np.concatenate([np.arange(0, 32), np.arange(64, 96), np.arange(32, 64), np.arange(96, 128)])
    return np.concatenate([base + LANES * b for b in range(AQ // LANES)])


def _rope_tables():
    pos = jnp.arange(SEQ, dtype=F32)
    ret_inv = 1.0 / (ROPE_THETA ** jnp.linspace(0.0, 1.0, RET_DK // 2, dtype=F32))
    ang = pos[:, None] * ret_inv[None, :]
    ret_cos = jnp.concatenate([jnp.cos(ang), jnp.cos(ang)], axis=-1)
    ret_sin = jnp.concatenate([-jnp.sin(ang), jnp.sin(ang)], axis=-1)
    att_inv = ROPE_THETA ** (-jnp.arange(0, ATT_DH, 2, dtype=F32) / ATT_DH)
    ang = pos[:, None] * att_inv[None, :]
    c, s = jnp.cos(ang), jnp.sin(ang)
    att_cos = jnp.concatenate([c, c, c, c], axis=-1)
    att_sin = jnp.concatenate([-s, -s, s, s], axis=-1)
    att_tabs = []
    for dil in ATT_DILATIONS:
        for tab in (att_cos, att_sin):
            att_tabs.append(tab.reshape(SEQ // dil, dil, LANES).transpose(1, 0, 2))
    return ret_cos, ret_sin, att_tabs


def _encoder(x, p):
    batch = x.shape[0]
    x2d = x.reshape(batch * SEQ, D_MODEL)
    rq, rk, rv, rg, gates = _proj_ret(x2d, p["norm_mix"], p["w_ret"], p["ret_cos"], p["ret_sin"])
    qkv = _proj_att(x2d, p["norm_mix"], p["w_att"], p["att_tabs"], batch)
    a = _retention(p["lgf"], p["lgb"], rq, rk, rv, rg, p["ret_norm"], batch)
    att = _attention(qkv, batch)
    x1, h2, aff_t = _mix_out(a, att, gates, x2d, p["w_ret_out"], p["w_att_out"], p["w_out"], p["norm_ffn"],
                             p["router_hi"], p["router_lo"])
    return _moe(x1, h2, aff_t, p).reshape(batch, SEQ, D_MODEL)


def kernel(x_prompt, x_sample, norm_mix, w_in, ret_log_decay_fwd, ret_log_decay_bwd, ret_norm, w_ret_out,
           w_att_out, w_out, norm_ffn, w_router, w_exp_gate, w_exp_up, w_exp_down, norm_final):
    w = w_in[0]
    c_rv = 2 * RQ
    c_rg = c_rv + RV
    c_aq = c_rg + RV
    c_ak = c_aq + AQ
    c_av = c_ak + AQ
    c_gate = c_av + AQ
    perm = _pair_perm()
    w_aq = w[:, c_aq:c_ak][:, perm] * (ATT_DH ** -0.5)
    w_ak = w[:, c_ak:c_av][:, perm]
    w_av = w[:, c_av:c_gate]
    ret_cos, ret_sin, att_tabs = _rope_tables()
    router_t = w_router[0].T
    router_hi = router_t.astype(BF16)
    p = {
        "norm_mix": norm_mix,
        "w_ret": jnp.concatenate([w[:, :c_aq], w[:, c_gate:]], axis=1).astype(BF16),
        "w_att": jnp.concatenate([w_aq, w_ak, w_av], axis=1).astype(BF16),
        "ret_cos": ret_cos, "ret_sin": ret_sin, "att_tabs": att_tabs,
        "lgf": ret_log_decay_fwd[0], "lgb": ret_log_decay_bwd[0],
        "ret_norm": ret_norm,
        "w_ret_out": w_ret_out[0].astype(BF16), "w_att_out": w_att_out[0].astype(BF16),
        "w_out": w_out[0].astype(BF16),
        "norm_ffn": norm_ffn,
        "router_hi": router_hi, "router_lo": (router_t - router_hi.astype(F32)).astype(BF16),
        "w_exp_gate": w_exp_gate[0].astype(BF16), "w_exp_up": w_exp_up[0].astype(BF16),
        "w_exp_down": w_exp_down[0].astype(BF16),
        "norm_final": norm_final.reshape(1, D_MODEL),
    }
    return _encoder(x_prompt, p), _encoder(x_sample, p)
```

```python
import functools

import numpy as np
import jax
import jax.numpy as jnp
from jax import lax
from jax.experimental import pallas as pl
from jax.experimental.pallas import tpu as pltpu

F32 = jnp.float32
BF16 = jnp.bfloat16

D_MODEL = 1024
SEQ = 4096
RET_HEADS = 4
RET_DK = 128
RET_DV = 256
RET_CHUNK = 256
RET_UNROLL = 16
RET_FIN = 512
ATT_DILATIONS = (1, 4, 16)
ATT_HALF_SPAN = 64
ATT_HEADS_PER_GROUP = 8
ATT_DH = 64
ATT_PAIRS = ATT_HEADS_PER_GROUP // 2
N_EXPERTS = 16
EC_CAPACITY_FACTOR = 2
EXPERT_HIDDEN = 2048
ROPE_THETA = 10000.0
EPS = 1e-6
RQ = RET_HEADS * RET_DK
RV = RET_HEADS * RET_DV
AQ = 3 * ATT_HEADS_PER_GROUP * ATT_DH
A_OUT = ATT_HEADS_PER_GROUP * ATT_DH
LANES = 128

PROJ_TM = 1024
ATT_QB = 128
ATT_KW = ATT_QB + 2 * ATT_HALF_SPAN
ATT_UNROLL = 32
FFN_TS = 256
TOK_ROW = D_MODEL + LANES
TOPK_SL = 512
TOPK_GROUP = 2
TOPK_TILES = 2
CMB_TM = 512
CMB_WIN = 112
CMB_ALIGN = 16
MASK_NEG = -1e30
VMEM_LIMIT = 56 * 1024 * 1024


def _params(*sem):
    return pltpu.CompilerParams(dimension_semantics=sem, vmem_limit_bytes=VMEM_LIMIT)


def _resident(shape, index_map):
    return pl.BlockSpec(shape, index_map, pipeline_mode=pl.Buffered(1))


def _sigmoid(x):
    return 1.0 / (1.0 + jnp.exp(-x))


def _rms_scale(x):
    return x * lax.rsqrt(jnp.mean(x * x, axis=-1, keepdims=True) + EPS)


def _rope(a, cos, sin):
    return a * cos + pltpu.roll(a, 64, 1) * sin


def _proj_ret_kernel(x_ref, nm_ref, w_ref, cos_ref, sin_ref, rq_ref, rk_ref, rv_ref, rg_ref, g_ref):
    h = (_rms_scale(x_ref[...]) * nm_ref[...]).astype(BF16)
    cos = cos_ref[...]
    sin = sin_ref[...]

    def proj(c0, width):
        return jnp.dot(h, w_ref[:, c0:c0 + width], preferred_element_type=F32)

    acc = proj(0, RQ)
    for hh in range(RET_HEADS):
        sl = slice(hh * LANES, (hh + 1) * LANES)
        rq_ref[:, sl] = _rope(acc[:, sl], cos, sin).astype(BF16)
    acc = proj(RQ, RQ)
    for hh in range(RET_HEADS):
        sl = slice(hh * LANES, (hh + 1) * LANES)
        rk_ref[:, sl] = (_rope(acc[:, sl], cos, sin) * (RET_DK ** -0.5)).astype(BF16)
    for c in range(RV // 512):
        rv_ref[:, c * 512:(c + 1) * 512] = proj(2 * RQ + c * 512, 512).astype(BF16)
    for c in range(RV // 512):
        acc = proj(2 * RQ + RV + c * 512, 512)
        rg_ref[:, c * 512:(c + 1) * 512] = (acc * _sigmoid(acc)).astype(BF16)
    for c in range(2 * D_MODEL // 512):
        acc = proj(2 * RQ + 2 * RV + c * 512, 512)
        g_ref[:, c * 512:(c + 1) * 512] = _sigmoid(acc).astype(BF16)


def _proj_ret(x2d, nm, w, cos, sin):
    n = x2d.shape[0]
    nt = SEQ // PROJ_TM
    tm = PROJ_TM
    row = lambda i: (i, 0)
    fixed = lambda i: (0, 0)
    tab = lambda i: (i % nt, 0)
    return pl.pallas_call(
        _proj_ret_kernel,
        grid=(n // tm,),
        in_specs=[pl.BlockSpec((tm, D_MODEL), row), pl.BlockSpec((1, D_MODEL), fixed),
                  _resident(w.shape, fixed), pl.BlockSpec((tm, LANES), tab), pl.BlockSpec((tm, LANES), tab)],
        out_specs=[pl.BlockSpec((tm, RQ), row), pl.BlockSpec((tm, RQ), row), pl.BlockSpec((tm, RV), row),
                   pl.BlockSpec((tm, RV), row), pl.BlockSpec((tm, 2 * D_MODEL), row)],
        out_shape=[jax.ShapeDtypeStruct((n, RQ), BF16), jax.ShapeDtypeStruct((n, RQ), BF16),
                   jax.ShapeDtypeStruct((n, RV), BF16), jax.ShapeDtypeStruct((n, RV), BF16),
                   jax.ShapeDtypeStruct((n, 2 * D_MODEL), BF16)],
        compiler_params=_params("parallel"),
        name="proj_ret",
    )(x2d, nm, w, cos, sin)


def _proj_att_kernel(x_ref, nm_ref, w_ref, c0_ref, s0_ref, c1_ref, s1_ref, c2_ref, s2_ref, *rest):
    out_refs = rest[:9]
    scr_ref = rest[9]
    h = (_rms_scale(x_ref[...]) * nm_ref[...]).astype(BF16)
    tabs = ((c0_ref, s0_ref), (c1_ref, s1_ref), (c2_ref, s2_ref))
    tm = x_ref.shape[0]
    for sec in range(3):
        for g, dil in enumerate(ATT_DILATIONS):
            c0 = (sec * 3 + g) * A_OUT
            acc = jnp.dot(h, w_ref[:, c0:c0 + A_OUT], preferred_element_type=F32)
            out_ref = out_refs[sec * 3 + g]
            if dil > 1:
                for p in range(ATT_PAIRS):
                    scr_ref[p] = acc[:, p * LANES:(p + 1) * LANES]
            for r in range(dil):
                for p in range(ATT_PAIRS):
                    if dil == 1:
                        a = acc[:, p * LANES:(p + 1) * LANES]
                    else:
                        a = scr_ref[p, pl.ds(r, tm // dil, stride=dil), :]
                    if sec < 2:
                        a = _rope(a, tabs[g][0][r], tabs[g][1][r])
                    out_ref[0, p, r] = a.astype(BF16)


def _proj_att(x2d, nm, w, tabs, batch):
    n = x2d.shape[0]
    tm = PROJ_TM
    nt = SEQ // tm
    row = lambda i: (i, 0)
    fixed = lambda i: (0, 0)
    in_specs = [pl.BlockSpec((tm, D_MODEL), row), pl.BlockSpec((1, D_MODEL), fixed), _resident(w.shape, fixed)]
    for dil in ATT_DILATIONS:
        for _ in range(2):
            in_specs.append(pl.BlockSpec((dil, tm // dil, LANES), lambda i: (0, i % nt, 0)))
    out_specs, out_shape = [], []
    for _ in range(3):
        for dil in ATT_DILATIONS:
            out_specs.append(pl.BlockSpec((1, ATT_PAIRS, dil, tm // dil, LANES),
                                          lambda i: (i // nt, 0, 0, i % nt, 0)))
            out_shape.append(jax.ShapeDtypeStruct((batch, ATT_PAIRS, dil, SEQ // dil, LANES), BF16))
    return pl.pallas_call(
        _proj_att_kernel,
        grid=(n // tm,),
        in_specs=in_specs,
        out_specs=out_specs,
        out_shape=out_shape,
        scratch_shapes=[pltpu.VMEM((ATT_PAIRS, tm, LANES), F32)],
        compiler_params=_params("parallel"),
        name="proj_att",
    )(x2d, nm, w, *tabs)


def _retention_kernel(lgf_ref, lgb_ref, q_ref, k_ref, v_ref, rg_ref, nrm_ref, o_ref, part_ref, cross_ref):
    hd = pl.program_id(1)
    lgf = lgf_ref[hd]
    lgb = lgb_ref[hd]
    c = RET_CHUNK
    nc = q_ref.shape[0] // c
    ti = lax.broadcasted_iota(jnp.int32, (c, c), 0)
    si = lax.broadcasted_iota(jnp.int32, (c, c), 1)
    diff = (ti - si).astype(F32)
    fwd = diff >= 0.0
    decay = jnp.where(fwd, jnp.exp(lgf * jnp.where(fwd, diff, 0.0)), jnp.exp(lgb * jnp.where(fwd, 0.0, -diff)))
    idx = lax.broadcasted_iota(jnp.int32, (c, 1), 0).astype(F32)
    qf_dec = jnp.exp(lgf * (idx + 1.0))
    kf_dec = jnp.exp(lgf * (c - 1.0 - idx))
    qb_dec = jnp.exp(lgb * (c - idx))
    kb_dec = jnp.exp(lgb * idx)
    one = jnp.ones((1, 1), F32)
    cdf = jnp.exp(one * (lgf * c))
    cdb = jnp.exp(one * (lgb * c))
    nrm = nrm_ref[...]
    tn = (((0,), (0,)), ((), ()))
    nt = (((1,), (1,)), ((), ()))

    def step(j, states):
        for u in range(RET_UNROLL):
            states = one_chunk(j * RET_UNROLL + u, states)
        return states

    def one_chunk(i, states):
        sf, sb = states
        rows = pl.ds(pl.multiple_of(i * c, c), c)
        q = q_ref[rows, :]
        k = k_ref[rows, :]
        v = v_ref[rows, :]
        scores = lax.dot_general(q, k, nt, preferred_element_type=F32) * decay
        part = jnp.dot(scores.astype(BF16), v, preferred_element_type=F32)
        part += jnp.dot((q.astype(F32) * qf_dec).astype(BF16), sf.astype(BF16), preferred_element_type=F32)
        part_ref[rows, :] = part
        sf = sf * cdf + lax.dot_general((k.astype(F32) * kf_dec).astype(BF16), v, tn, preferred_element_type=F32)

        rows = pl.ds(pl.multiple_of((nc - 1 - i) * c, c), c)
        q = q_ref[rows, :].astype(F32)
        k = k_ref[rows, :].astype(F32)
        cross_ref[rows, :] = jnp.dot((q * qb_dec).astype(BF16), sb.astype(BF16), preferred_element_type=F32)
        sb = sb * cdb + lax.dot_general((k * kb_dec).astype(BF16), v_ref[rows, :], tn, preferred_element_type=F32)
        return sf, sb

    zero = jnp.zeros((RET_DK, RET_DV), F32)
    lax.fori_loop(0, nc // RET_UNROLL, step, (zero, zero))

    def finish(i, carry):
        rows = pl.ds(pl.multiple_of(i * RET_FIN, RET_FIN), RET_FIN)
        out = part_ref[rows, :] + cross_ref[rows, :]
        o_ref[rows, :] = (rg_ref[rows, :].astype(F32) * (_rms_scale(out) * nrm)).astype(BF16)
        return carry

    lax.fori_loop(0, q_ref.shape[0] // RET_FIN, finish, 0)


def _retention(lgf, lgb, rq, rk, rv, rg, nrm, batch):
    n = rq.shape[0]
    bh = lambda b, h, *_: (b, h)
    return pl.pallas_call(
        _retention_kernel,
        grid_spec=pltpu.PrefetchScalarGridSpec(
            num_scalar_prefetch=2,
            grid=(batch, RET_HEADS),
            in_specs=[pl.BlockSpec((SEQ, RET_DK), bh), pl.BlockSpec((SEQ, RET_DK), bh),
                      pl.BlockSpec((SEQ, RET_DV), bh), pl.BlockSpec((SEQ, RET_DV), bh),
                      pl.BlockSpec((1, RET_DV), lambda b, h, *_: (0, h))],
            out_specs=pl.BlockSpec((SEQ, RET_DV), bh),
            scratch_shapes=[pltpu.VMEM((SEQ, RET_DV), F32), pltpu.VMEM((SEQ, RET_DV), F32)],
        ),
        out_shape=jax.ShapeDtypeStruct((n, RV), BF16),
        compiler_params=_params("parallel", "parallel"),
        name="retention",
    )(lgf, lgb, rq, rk, rv, rg, nrm)


def _attn_block(q, kw, vw, bias):
    lane = lax.broadcasted_iota(jnp.int32, q.shape, 1)
    head0_qk = (lane < 32) | ((lane >= 64) & (lane < 96))
    outs, maxs, sums = [], [], []
    for hh in range(2):
        qm = jnp.where(head0_qk if hh == 0 else jnp.logical_not(head0_qk), q, jnp.zeros_like(q))
        s = lax.dot_general(qm, kw, (((1,), (1,)), ((), ())), preferred_element_type=F32) + bias
        m = jnp.max(s, axis=-1, keepdims=True)
        p = jnp.exp(s - m)
        outs.append(jnp.dot(p.astype(BF16), vw, preferred_element_type=F32))
        maxs.append(m)
        sums.append(jnp.sum(p, axis=-1, keepdims=True))
    head0_v = lane < ATT_DH
    return (jnp.where(head0_v, outs[0], outs[1]), jnp.where(head0_v, maxs[0], maxs[1]),
            jnp.where(head0_v, sums[0], sums[1]))


def _attention_kernel(q0, k0, v0, q1, k1, v1, q2, k2, v2, o_ref, o1_ref, m1_ref, l1_ref, o2_ref, m2_ref, l2_ref,
                      bias_ref):
    qb, kw = ATT_QB, ATT_KW
    ji = lax.broadcasted_iota(jnp.int32, (qb, kw), 1) - lax.broadcasted_iota(jnp.int32, (qb, kw), 0)
    for n in range(3):
        rel = ji - n * ATT_HALF_SPAN
        bias_ref[n] = jnp.where((rel >= -ATT_HALF_SPAN) & (rel <= ATT_HALF_SPAN), 0.0, MASK_NEG).astype(F32)

    def group(q_ref, k_ref, v_ref, dil, finish):
        sub_len = SEQ // dil
        nblk = sub_len // qb

        def body(jt, carry):
            for u in range(ATT_UNROLL):
                it = jt * ATT_UNROLL + u
                r = it // nblk
                l0 = (it % nblk) * qb
                kstart = pl.multiple_of(jnp.clip(l0 - ATT_HALF_SPAN, 0, sub_len - kw), ATT_HALF_SPAN)
                bias = bias_ref[(l0 - kstart) // ATT_HALF_SPAN]
                q = q_ref[0, 0, r, pl.ds(pl.multiple_of(l0, qb), qb), :]
                kwin = k_ref[0, 0, r, pl.ds(kstart, kw), :]
                vwin = v_ref[0, 0, r, pl.ds(kstart, kw), :]
                finish(l0, r, *_attn_block(q, kwin, vwin, bias))
            return carry

        lax.fori_loop(0, dil * nblk // ATT_UNROLL, body, 0)

    def store_strided(o_nat, m_nat, l_nat, dil):
        def finish(l0, r, o, m, l):
            rows = pl.ds(l0 * dil + r, ATT_QB, stride=dil)
            o_nat[rows, :] = o
            m_nat[rows, :] = m
            l_nat[rows, :] = l
        return finish

    group(q2, k2, v2, ATT_DILATIONS[2], store_strided(o2_ref, m2_ref, l2_ref, ATT_DILATIONS[2]))
    group(q1, k1, v1, ATT_DILATIONS[1], store_strided(o1_ref, m1_ref, l1_ref, ATT_DILATIONS[1]))

    def merge(l0, r, o, m, l):
        rows = pl.ds(pl.multiple_of(l0, ATT_QB), ATT_QB)
        ma, mb = m1_ref[rows, :], m2_ref[rows, :]
        mx = jnp.maximum(m, jnp.maximum(ma, mb))
        w0, w1, w2 = jnp.exp(m - mx), jnp.exp(ma - mx), jnp.exp(mb - mx)
        num = w0 * o + w1 * o1_ref[rows, :] + w2 * o2_ref[rows, :]
        den = w0 * l + w1 * l1_ref[rows, :] + w2 * l2_ref[rows, :]
        o_ref[rows, :] = (num / den).astype(BF16)

    group(q0, k0, v0, ATT_DILATIONS[0], merge)


def _attention(qkv, batch):
    in_specs = []
    for g, dil in enumerate(ATT_DILATIONS):
        for _ in range(3):
            in_specs.append(pl.BlockSpec((1, 1, dil, SEQ // dil, LANES), lambda b, p: (b, p, 0, 0, 0)))
    args = []
    for g in range(3):
        args += [qkv[0 * 3 + g], qkv[1 * 3 + g], qkv[2 * 3 + g]]
    return pl.pallas_call(
        _attention_kernel,
        grid=(batch, ATT_PAIRS),
        in_specs=in_specs,
        out_specs=pl.BlockSpec((SEQ, LANES), lambda b, p: (b, p)),
        out_shape=jax.ShapeDtypeStruct((batch * SEQ, A_OUT), BF16),
        scratch_shapes=[pltpu.VMEM((SEQ, LANES), F32) for _ in range(6)]
        + [pltpu.VMEM((3, ATT_QB, ATT_KW), F32)],
        compiler_params=_params("parallel", "parallel"),
        name="dilated_attention",
    )(*args)


def _mix_out_kernel(a_ref, att_ref, g_ref, x_ref, wr_ref, wa_ref, wo_ref, nf_ref, rhi_ref, rlo_ref,
                    x1_ref, h2_ref, aff_ref):
    bra = jnp.dot(a_ref[...], wr_ref[...], preferred_element_type=F32)
    brb = jnp.dot(att_ref[...], wa_ref[...], preferred_element_type=F32)
    g = g_ref[...].astype(F32)
    merged = g[:, :D_MODEL] * bra + g[:, D_MODEL:] * brb
    x1 = x_ref[...] + jnp.dot(merged.astype(BF16), wo_ref[...], preferred_element_type=F32)
    x1_ref[...] = x1
    h2 = _rms_scale(x1) * nf_ref[...]
    hi = h2.astype(BF16)
    lo = (h2 - hi.astype(F32)).astype(BF16)
    nt = (((1,), (1,)), ((), ()))
    logits = (lax.dot_general(rhi_ref[...], hi, nt, preferred_element_type=F32)
              + lax.dot_general(rhi_ref[...], lo, nt, preferred_element_type=F32)
              + lax.dot_general(rlo_ref[...], hi, nt, preferred_element_type=F32))
    e = jnp.exp(logits - jnp.max(logits, axis=0, keepdims=True))
    aff = e / jnp.sum(e, axis=0, keepdims=True)
    aff_ref[...] = aff
    h2_ref[:, :D_MODEL] = h2
    pad = jnp.zeros((LANES - N_EXPERTS, aff.shape[1]), F32)
    h2_ref[:, D_MODEL:] = jnp.transpose(jnp.concatenate([aff, pad], axis=0))


def _mix_out(a, att, g, x2d, wr, wa, wo, nf, rhi, rlo):
    n = x2d.shape[0]
    tm = PROJ_TM
    row = lambda i: (i, 0)
    fixed = lambda i: (0, 0)
    return pl.pallas_call(
        _mix_out_kernel,
        grid=(n // tm,),
        in_specs=[pl.BlockSpec((tm, RV), row), pl.BlockSpec((tm, A_OUT), row), pl.BlockSpec((tm, 2 * D_MODEL), row),
                  pl.BlockSpec((tm, D_MODEL), row), _resident(wr.shape, fixed), _resident(wa.shape, fixed),
                  _resident(wo.shape, fixed), pl.BlockSpec((1, D_MODEL), fixed),
                  pl.BlockSpec(rhi.shape, fixed), pl.BlockSpec(rlo.shape, fixed)],
        out_specs=[pl.BlockSpec((tm, D_MODEL), row), pl.BlockSpec((tm, TOK_ROW), row),
                   pl.BlockSpec((N_EXPERTS, tm), lambda i: (0, i))],
        out_shape=[jax.ShapeDtypeStruct((n, D_MODEL), F32), jax.ShapeDtypeStruct((n, TOK_ROW), F32),
                   jax.ShapeDtypeStruct((N_EXPERTS, n), F32)],
        compiler_params=_params("parallel"),
        name="mix_out",
    )(a, att, g, x2d, wr, wa, wo, nf, rhi, rlo)


def _topk_kernel(aff_ref, spos_ref, offs_ref, idx_ref, *, cap):
    nchunk = aff_ref.shape[1]
    sl = TOPK_SL
    li = lax.broadcasted_iota(jnp.int32, (LANES, LANES), 0)
    lj = lax.broadcasted_iota(jnp.int32, (LANES, LANES), 1)
    upper = jnp.where(li <= lj, 1.0, 0.0).astype(BF16)
    ci = lax.broadcasted_iota(jnp.int32, (nchunk, nchunk), 0)
    cj = lax.broadcasted_iota(jnp.int32, (nchunk, nchunk), 1)
    lower = jnp.where(cj < ci, 1.0, 0.0).astype(BF16)
    row_ids = lax.broadcasted_iota(jnp.int32, (sl, nchunk), 1).astype(F32)
    slot_iota = lax.broadcasted_iota(jnp.int32, (sl, 1), 0).astype(F32)

    def total(x):
        return jnp.sum(jnp.sum(x, axis=0, keepdims=True), axis=1, keepdims=True)

    def cumsum(mask):
        within = jnp.dot(mask.astype(BF16), upper, preferred_element_type=F32)
        tot = jnp.broadcast_to(within[:, LANES - 1:LANES], within.shape).astype(BF16)
        before = jnp.dot(lower, tot, preferred_element_type=F32)
        return within + before, before

    def expert_group(g, carry):
        experts = [g * TOPK_GROUP + u for u in range(TOPK_GROUP)]
        all_bits = [pltpu.bitcast(aff_ref[e], jnp.int32) for e in experts]

        def search(i, thrs):
            out = []
            for bits, thr in zip(all_bits, thrs):
                cand = thr | (1 << (30 - i))
                cnt = total(jnp.where(bits >= cand, 1, 0))
                out.append(jnp.where(cnt >= cap, cand, thr))
            return tuple(out)

        thrs = lax.fori_loop(0, 31, search, tuple(jnp.zeros((1, 1), jnp.int32) for _ in experts))
        inverters = [select(e, bits, thr) for e, bits, thr in zip(experts, all_bits, thrs)]

        def slot_tiles(jj, carry2):
            for u in range(TOPK_TILES):
                for invert in inverters:
                    invert(jj * TOPK_TILES + u)
            return carry2

        lax.fori_loop(0, cap // sl // TOPK_TILES, slot_tiles, 0)
        return carry

    def select(e, bits, thr):
        gt = bits > thr
        eq = bits == thr
        need = (cap - total(jnp.where(gt, 1, 0))).astype(F32)
        eq_f = jnp.where(eq, 1.0, 0.0)
        eq_cum, _ = cumsum(eq_f)
        sel = gt | (eq & (eq_cum - eq_f < need))
        cum, before = cumsum(jnp.where(sel, 1.0, 0.0))
        spos_ref[e] = jnp.where(sel, cum - 1.0, -1.0).astype(jnp.int32)
        offs_ref[e] = before.astype(jnp.int32)
        in_row = (cum - before).astype(BF16)
        ends = jnp.transpose(jnp.broadcast_to(cum[:, LANES - 1:LANES], cum.shape))[0:1, :]

        def slot_tile(jt):
            j = slot_iota + lax.convert_element_type(jt * sl, F32)
            done = ends <= j
            c_j = jnp.sum(jnp.where(done, 1.0, 0.0), axis=1, keepdims=True)
            before_j = jnp.max(jnp.where(done, ends, 0.0), axis=1, keepdims=True)
            pick = jnp.where(row_ids == c_j, 1.0, 0.0).astype(BF16)
            counts = jnp.dot(pick, in_row, preferred_element_type=F32)
            lane_j = jnp.sum(jnp.where(counts <= j - before_j, 1.0, 0.0), axis=1, keepdims=True)
            tok = c_j * float(LANES) + lane_j
            idx_ref[e, pl.ds(jt, 1), :] = jnp.transpose(jnp.broadcast_to(tok, (sl, LANES)))[0:1, :].astype(jnp.int32)

        return slot_tile

    lax.fori_loop(0, N_EXPERTS // TOPK_GROUP, expert_group, 0)


def _topk(aff3, cap):
    e, nchunk, _ = aff3.shape
    assert e % TOPK_GROUP == 0 and cap % (TOPK_SL * TOPK_TILES) == 0
    return pl.pallas_call(
        functools.partial(_topk_kernel, cap=cap),
        out_shape=[jax.ShapeDtypeStruct((e, nchunk, LANES), jnp.int32),
                   jax.ShapeDtypeStruct((e, nchunk, LANES), jnp.int32),
                   jax.ShapeDtypeStruct((e, cap // TOPK_SL, TOPK_SL), jnp.int32)],
        compiler_params=pltpu.CompilerParams(vmem_limit_bytes=VMEM_LIMIT),
        name="expert_choice_topk",
    )(aff3)


def _ffn_kernel(idx_ref, tok_hbm, wg_ref, wu_ref, wd_ref, out_ref, xs_ref, sem_ref):
    e = pl.program_id(0)
    s = pl.program_id(1)
    n_steps = pl.num_programs(0) * pl.num_programs(1)
    step = e * pl.num_programs(1) + s
    ts = FFN_TS

    def gather_rows(step_, buf, r0, r1):
        for r in range(r0, r1):
            pltpu.make_async_copy(tok_hbm.at[pl.ds(idx_ref[step_ * ts + r], 1), :],
                                  xs_ref.at[buf, pl.ds(r, 1), :], sem_ref.at[buf]).start()

    def wait_rows(buf):
        pltpu.make_async_copy(tok_hbm.at[pl.ds(0, ts), :], xs_ref.at[buf], sem_ref.at[buf]).wait()

    @pl.when(step == 0)
    def _():
        gather_rows(0, 0, 0, ts)

    def tile(buf):
        wait_rows(buf)
        gather_rows(jnp.minimum(step + 1, n_steps - 1), 1 - buf, 0, ts)
        xs = xs_ref[buf, :, :D_MODEL].astype(BF16)
        aff = xs_ref[buf, :, D_MODEL:]
        lane = lax.broadcasted_iota(jnp.int32, aff.shape, 1)
        route = jnp.sum(jnp.where(lane == e, aff, 0.0), axis=1, keepdims=True)
        gate = jnp.dot(xs, wg_ref[0], preferred_element_type=F32)
        up = jnp.dot(xs, wu_ref[0], preferred_element_type=F32)
        hid = (gate * _sigmoid(gate) * up).astype(BF16)
        out_ref[0] = (jnp.dot(hid, wd_ref[0], preferred_element_type=F32) * route).astype(BF16)

        @pl.when(step == n_steps - 1)
        def _():
            wait_rows(1 - buf)

    for parity in range(2):
        pl.when(step % 2 == parity)(functools.partial(tile, parity))


def _ffn(idx, tok, wg, wu, wd, cap):
    n_tiles = cap // FFN_TS
    wmap = lambda e, s, *_: (e, 0, 0)
    return pl.pallas_call(
        _ffn_kernel,
        grid_spec=pltpu.PrefetchScalarGridSpec(
            num_scalar_prefetch=1,
            grid=(N_EXPERTS, n_tiles),
            in_specs=[pl.BlockSpec(memory_space=pl.ANY),
                      pl.BlockSpec((1, D_MODEL, EXPERT_HIDDEN), wmap),
                      pl.BlockSpec((1, D_MODEL, EXPERT_HIDDEN), wmap),
                      pl.BlockSpec((1, EXPERT_HIDDEN, D_MODEL), wmap)],
            out_specs=pl.BlockSpec((1, FFN_TS, D_MODEL), lambda e, s, *_: (e, s, 0)),
            scratch_shapes=[pltpu.VMEM((2, FFN_TS, TOK_ROW), F32), pltpu.SemaphoreType.DMA((2,))],
        ),
        out_shape=jax.ShapeDtypeStruct((N_EXPERTS, cap, D_MODEL), BF16),
        compiler_params=_params("arbitrary", "arbitrary"),
        name="expert_ffn",
    )(idx, tok, wg, wu, wd)


def _combine_kernel(base_ref, spos_ref, x1_ref, nf_ref, eo_hbm, o_ref, stage_ref, sem_ref, tail_ref, tail_sem,
                    acc_ref, *, cap):
    i = pl.program_id(0)
    tm, win = CMB_TM, CMB_WIN
    row_ids = lax.broadcasted_iota(jnp.int32, (win, tm), 0)
    tn = (((0,), (0,)), ((), ()))

    def window_start(tile, e):
        return jnp.minimum((base_ref[tile * N_EXPERTS + e] // CMB_ALIGN) * CMB_ALIGN, cap - win)

    def window_copy(tile, e, buf):
        a0 = pl.multiple_of(window_start(tile, e), CMB_ALIGN)
        return pltpu.make_async_copy(eo_hbm.at[e, pl.ds(a0, win), :], stage_ref.at[buf, pl.ds(e * win, win), :],
                                     sem_ref.at[buf, e])

    @pl.when(i == 0)
    def _():
        for e in range(N_EXPERTS):
            window_copy(0, e, 0).start()

    @pl.when(i + 1 < pl.num_programs(0))
    def _():
        for e in range(N_EXPERTS):
            window_copy(i + 1, e, (i + 1) % 2).start()

    buf = i % 2
    pieces = []
    for e in range(N_EXPERTS):
        window_copy(i, e, buf).wait()
        pieces.append(jnp.where(spos_ref[e:e + 1, :] == window_start(i, e) + row_ids, 1.0, 0.0).astype(BF16))
    pick = jnp.concatenate(pieces, axis=0)
    acc_ref[...] = x1_ref[...] + lax.dot_general(pick, stage_ref[buf], tn, preferred_element_type=F32)

    for e in range(N_EXPERTS):
        end = base_ref[(i + 1) * N_EXPERTS + e]
        covered = window_start(i, e) + win

        @pl.when(end > covered)
        def _(e=e, end=end, covered=covered):
            def extra(w, carry):
                lo = covered + w * win
                a0 = pl.multiple_of(jnp.minimum(lo, cap - win), CMB_ALIGN)
                cp = pltpu.make_async_copy(eo_hbm.at[e, pl.ds(a0, win), :], tail_ref, tail_sem)
                cp.start()
                cp.wait()
                pos = spos_ref[e:e + 1, :]
                hit = (pos == a0 + row_ids) & (pos >= lo)
                acc_ref[...] += lax.dot_general(jnp.where(hit, 1.0, 0.0).astype(BF16), tail_ref[...], tn,
                                                preferred_element_type=F32)
                return carry

            lax.fori_loop(0, (end - covered + win - 1) // win, extra, 0)

    o_ref[...] = _rms_scale(acc_ref[...]) * nf_ref[...]


def _combine(base, spos, x1, nf, eo):
    n = x1.shape[0]
    cap = eo.shape[1]
    tm = CMB_TM
    row = lambda i, *_: (i, 0)
    return pl.pallas_call(
        functools.partial(_combine_kernel, cap=cap),
        grid_spec=pltpu.PrefetchScalarGridSpec(
            num_scalar_prefetch=1,
            grid=(n // tm,),
            in_specs=[pl.BlockSpec((N_EXPERTS, tm), lambda i, *_: (0, i)),
                      pl.BlockSpec((tm, D_MODEL), row), pl.BlockSpec((1, D_MODEL), lambda i, *_: (0, 0)),
                      pl.BlockSpec(memory_space=pl.ANY)],
            out_specs=pl.BlockSpec((tm, D_MODEL), row),
            scratch_shapes=[pltpu.VMEM((2, N_EXPERTS * CMB_WIN, D_MODEL), BF16),
                            pltpu.SemaphoreType.DMA((2, N_EXPERTS)),
                            pltpu.VMEM((CMB_WIN, D_MODEL), BF16), pltpu.SemaphoreType.DMA,
                            pltpu.VMEM((tm, D_MODEL), F32)],
        ),
        out_shape=jax.ShapeDtypeStruct((n, D_MODEL), F32),
        compiler_params=_params("arbitrary"),
        name="combine_final",
    )(base, spos, x1, nf, eo)


def _moe(x1, h2, aff_t, p):
    n = x1.shape[0]
    cap = EC_CAPACITY_FACTOR * n // N_EXPERTS
    spos, offs, idx = _topk(aff_t.reshape(N_EXPERTS, n // LANES, LANES), cap)
    eo = _ffn(idx.reshape(-1), h2, p["w_exp_gate"], p["w_exp_up"], p["w_exp_down"], cap)
    base = offs[:, ::CMB_TM // LANES, 0].T
    base = jnp.concatenate([base, jnp.full((1, N_EXPERTS), cap, jnp.int32)], axis=0).reshape(-1)
    return _combine(base, spos.reshape(N_EXPERTS, n), x1, p["norm_final"], eo)


def _pair_perm():
    base = np.concatenate([np.arange(0, 32), np.arange(64, 96), np.arange(32, 64), np.arange(96, 128)])
    return np.concatenate([base + LANES * b for b in range(AQ // LANES)])


def _rope_tables():
    pos = jnp.arange(SEQ, dtype=F32)
    ret_inv = 1.0 / (ROPE_THETA ** jnp.linspace(0.0, 1.0, RET_DK // 2, dtype=F32))
    ang = pos[:, None] * ret_inv[None, :]
    ret_cos = jnp.concatenate([jnp.cos(ang), jnp.cos(ang)], axis=-1)
    ret_sin = jnp.concatenate([-jnp.sin(ang), jnp.sin(ang)], axis=-1)
    att_inv = ROPE_THETA ** (-jnp.arange(0, ATT_DH, 2, dtype=F32) / ATT_DH)
    ang = pos[:, None] * att_inv[None, :]
    c, s = jnp.cos(ang), jnp.sin(ang)
    att_cos = jnp.concatenate([c, c, c, c], axis=-1)
    att_sin = jnp.concatenate([-s, -s, s, s], axis=-1)
    att_tabs = []
    for dil in ATT_DILATIONS:
        for tab in (att_cos, att_sin):
            att_tabs.append(tab.reshape(SEQ // dil, dil, LANES).transpose(1, 0, 2))
    return ret_cos, ret_sin, att_tabs


def _encoder(x, p):
    batch = x.shape[0]
    x2d = x.reshape(batch * SEQ, D_MODEL)
    rq, rk, rv, rg, gates = _proj_ret(x2d, p["norm_mix"], p["w_ret"], p["ret_cos"], p["ret_sin"])
    qkv = _proj_att(x2d, p["norm_mix"], p["w_att"], p["att_tabs"], batch)
    a = _retention(p["lgf"], p["lgb"], rq, rk, rv, rg, p["ret_norm"], batch)
    att = _attention(qkv, batch)
    x1, h2, aff_t = _mix_out(a, att, gates, x2d, p["w_ret_out"], p["w_att_out"], p["w_out"], p["norm_ffn"],
                             p["router_hi"], p["router_lo"])
    return _moe(x1, h2, aff_t, p).reshape(batch, SEQ, D_MODEL)


def kernel(x_prompt, x_sample, norm_mix, w_in, ret_log_decay_fwd, ret_log_decay_bwd, ret_norm, w_ret_out,
           w_att_out, w_out, norm_ffn, w_router, w_exp_gate, w_exp_up, w_exp_down, norm_final):
    w = w_in[0]
    c_rv = 2 * RQ
    c_rg = c_rv + RV
    c_aq = c_rg + RV
    c_ak = c_aq + AQ
    c_av = c_ak + AQ
    c_gate = c_av + AQ
    perm = _pair_perm()
    w_aq = w[:, c_aq:c_ak][:, perm] * (ATT_DH ** -0.5)
    w_ak = w[:, c_ak:c_av][:, perm]
    w_av = w[:, c_av:c_gate]
    ret_cos, ret_sin, att_tabs = _rope_tables()
    router_t = w_router[0].T
    router_hi = router_t.astype(BF16)
    p = {
        "norm_mix": norm_mix,
        "w_ret": jnp.concatenate([w[:, :c_aq], w[:, c_gate:]], axis=1).astype(BF16),
        "w_att": jnp.concatenate([w_aq, w_ak, w_av], axis=1).astype(BF16),
        "ret_cos": ret_cos, "ret_sin": ret_sin, "att_tabs": att_tabs,
        "lgf": ret_log_decay_fwd[0], "lgb": ret_log_decay_bwd[0],
        "ret_norm": ret_norm,
        "w_ret_out": w_ret_out[0].astype(BF16), "w_att_out": w_att_out[0].astype(BF16),
        "w_out": w_out[0].astype(BF16),
        "norm_ffn": norm_ffn,
        "router_hi": router_hi, "router_lo": (router_t - router_hi.astype(F32)).astype(BF16),
        "w_exp_gate": w_exp_gate[0].astype(BF16), "w_exp_up": w_exp_up[0].astype(BF16),
        "w_exp_down": w_exp_down[0].astype(BF16),
        "norm_final": norm_final.reshape(1, D_MODEL),
    }
    return _encoder(x_prompt, p), _encoder(x_sample, p)
```

```python
import functools

import numpy as np
import jax
import jax.numpy as jnp
from jax import lax
from jax.experimental import pallas as pl
from jax.experimental.pallas import tpu as pltpu

F32 = jnp.float32
BF16 = jnp.bfloat16

D_MODEL = 1024
SEQ = 4096
RET_HEADS = 4
RET_DK = 128
RET_DV = 256
RET_CHUNK = 256
ATT_DILATIONS = (1, 4, 16)
ATT_HALF_SPAN = 64
ATT_HEADS_PER_GROUP = 8
ATT_DH = 64
ATT_PAIRS = ATT_HEADS_PER_GROUP // 2
N_EXPERTS = 16
EC_CAPACITY_FACTOR = 2
EXPERT_HIDDEN = 2048
ROPE_THETA = 10000.0
EPS = 1e-6
RQ = RET_HEADS * RET_DK
RV = RET_HEADS * RET_DV
AQ = 3 * ATT_HEADS_PER_GROUP * ATT_DH
A_OUT = ATT_HEADS_PER_GROUP * ATT_DH
LANES = 128

PROJ_TM = 1024
ATT_QB = 128
ATT_KW = ATT_QB + 2 * ATT_HALF_SPAN
ATT_UNROLL = 32
FFN_TS = 256
TOK_ROW = D_MODEL + LANES
TOPK_SL = 512
TOPK_GROUP = 4
TOPK_TILES = 1
CMB_TM = 512
CMB_WIN = 112
CMB_ALIGN = 16
MASK_NEG = -1e30
VMEM_LIMIT = 56 * 1024 * 1024


def _params(*sem):
    return pltpu.CompilerParams(dimension_semantics=sem, vmem_limit_bytes=VMEM_LIMIT)


def _resident(shape, index_map):
    return pl.BlockSpec(shape, index_map, pipeline_mode=pl.Buffered(1))


def _sigmoid(x):
    return 1.0 / (1.0 + jnp.exp(-x))


def _rms_scale(x):
    return x * lax.rsqrt(jnp.mean(x * x, axis=-1, keepdims=True) + EPS)


def _rope(a, cos, sin):
    return a * cos + pltpu.roll(a, 64, 1) * sin


def _proj_ret_kernel(x_ref, nm_ref, w_ref, cos_ref, sin_ref, rq_ref, rk_ref, rv_ref, rg_ref, g_ref):
    h = (_rms_scale(x_ref[...]) * nm_ref[...]).astype(BF16)
    cos = cos_ref[...]
    sin = sin_ref[...]

    def proj(c0, width):
        return jnp.dot(h, w_ref[:, c0:c0 + width], preferred_element_type=F32)

    acc = proj(0, RQ)
    for hh in range(RET_HEADS):
        sl = slice(hh * LANES, (hh + 1) * LANES)
        rq_ref[:, sl] = _rope(acc[:, sl], cos, sin).astype(BF16)
    acc = proj(RQ, RQ)
    for hh in range(RET_HEADS):
        sl = slice(hh * LANES, (hh + 1) * LANES)
        rk_ref[:, sl] = (_rope(acc[:, sl], cos, sin) * (RET_DK ** -0.5)).astype(BF16)
    for c in range(RV // 512):
        rv_ref[:, c * 512:(c + 1) * 512] = proj(2 * RQ + c * 512, 512).astype(BF16)
    for c in range(RV // 512):
        acc = proj(2 * RQ + RV + c * 512, 512)
        rg_ref[:, c * 512:(c + 1) * 512] = (acc * _sigmoid(acc)).astype(BF16)
    for c in range(2 * D_MODEL // 512):
        acc = proj(2 * RQ + 2 * RV + c * 512, 512)
        g_ref[:, c * 512:(c + 1) * 512] = _sigmoid(acc).astype(BF16)


def _proj_ret(x2d, nm, w, cos, sin):
    n = x2d.shape[0]
    nt = SEQ // PROJ_TM
    tm = PROJ_TM
    row = lambda i: (i, 0)
    fixed = lambda i: (0, 0)
    tab = lambda i: (i % nt, 0)
    return pl.pallas_call(
        _proj_ret_kernel,
        grid=(n // tm,),
        in_specs=[pl.BlockSpec((tm, D_MODEL), row), pl.BlockSpec((1, D_MODEL), fixed),
                  _resident(w.shape, fixed), pl.BlockSpec((tm, LANES), tab), pl.BlockSpec((tm, LANES), tab)],
        out_specs=[pl.BlockSpec((tm, RQ), row), pl.BlockSpec((tm, RQ), row), pl.BlockSpec((tm, RV), row),
                   pl.BlockSpec((tm, RV), row), pl.BlockSpec((tm, 2 * D_MODEL), row)],
        out_shape=[jax.ShapeDtypeStruct((n, RQ), BF16), jax.ShapeDtypeStruct((n, RQ), BF16),
                   jax.ShapeDtypeStruct((n, RV), BF16), jax.ShapeDtypeStruct((n, RV), BF16),
                   jax.ShapeDtypeStruct((n, 2 * D_MODEL), BF16)],
        compiler_params=_params("parallel"),
        name="proj_ret",
    )(x2d, nm, w, cos, sin)


def _proj_att_kernel(x_ref, nm_ref, w_ref, c0_ref, s0_ref, c1_ref, s1_ref, c2_ref, s2_ref, *rest):
    out_refs = rest[:9]
    scr_ref = rest[9]
    h = (_rms_scale(x_ref[...]) * nm_ref[...]).astype(BF16)
    tabs = ((c0_ref, s0_ref), (c1_ref, s1_ref), (c2_ref, s2_ref))
    tm = x_ref.shape[0]
    for sec in range(3):
        for g, dil in enumerate(ATT_DILATIONS):
            c0 = (sec * 3 + g) * A_OUT
            acc = jnp.dot(h, w_ref[:, c0:c0 + A_OUT], preferred_element_type=F32)
            out_ref = out_refs[sec * 3 + g]
            if dil > 1:
                for p in range(ATT_PAIRS):
                    scr_ref[p] = acc[:, p * LANES:(p + 1) * LANES]
            for r in range(dil):
                for p in range(ATT_PAIRS):
                    if dil == 1:
                        a = acc[:, p * LANES:(p + 1) * LANES]
                    else:
                        a = scr_ref[p, pl.ds(r, tm // dil, stride=dil), :]
                    if sec < 2:
                        a = _rope(a, tabs[g][0][r], tabs[g][1][r])
                    out_ref[0, p, r] = a.astype(BF16)


def _proj_att(x2d, nm, w, tabs, batch):
    n = x2d.shape[0]
    tm = PROJ_TM
    nt = SEQ // tm
    row = lambda i: (i, 0)
    fixed = lambda i: (0, 0)
    in_specs = [pl.BlockSpec((tm, D_MODEL), row), pl.BlockSpec((1, D_MODEL), fixed), _resident(w.shape, fixed)]
    for dil in ATT_DILATIONS:
        for _ in range(2):
            in_specs.append(pl.BlockSpec((dil, tm // dil, LANES), lambda i: (0, i % nt, 0)))
    out_specs, out_shape = [], []
    for _ in range(3):
        for dil in ATT_DILATIONS:
            out_specs.append(pl.BlockSpec((1, ATT_PAIRS, dil, tm // dil, LANES),
                                          lambda i: (i // nt, 0, 0, i % nt, 0)))
            out_shape.append(jax.ShapeDtypeStruct((batch, ATT_PAIRS, dil, SEQ // dil, LANES), BF16))
    return pl.pallas_call(
        _proj_att_kernel,
        grid=(n // tm,),
        in_specs=in_specs,
        out_specs=out_specs,
        out_shape=out_shape,
        scratch_shapes=[pltpu.VMEM((ATT_PAIRS, tm, LANES), F32)],
        compiler_params=_params("parallel"),
        name="proj_att",
    )(x2d, nm, w, *tabs)


def _retention_kernel(lgf_ref, lgb_ref, q_ref, k_ref, v_ref, rg_ref, nrm_ref, o_ref, part_ref, cross_ref):
    hd = pl.program_id(1)
    lgf = lgf_ref[hd]
    lgb = lgb_ref[hd]
    c = RET_CHUNK
    nc = q_ref.shape[0] // c
    ti = lax.broadcasted_iota(jnp.int32, (c, c), 0)
    si = lax.broadcasted_iota(jnp.int32, (c, c), 1)
    diff = (ti - si).astype(F32)
    fwd = diff >= 0.0
    decay = jnp.where(fwd, jnp.exp(lgf * jnp.where(fwd, diff, 0.0)), jnp.exp(lgb * jnp.where(fwd, 0.0, -diff)))
    idx = lax.broadcasted_iota(jnp.int32, (c, 1), 0).astype(F32)
    qf_dec = jnp.exp(lgf * (idx + 1.0))
    kf_dec = jnp.exp(lgf * (c - 1.0 - idx))
    qb_dec = jnp.exp(lgb * (c - idx))
    kb_dec = jnp.exp(lgb * idx)
    one = jnp.ones((1, 1), F32)
    cdf = jnp.exp(one * (lgf * c))
    cdb = jnp.exp(one * (lgb * c))
    nrm = nrm_ref[...]
    tn = (((0,), (0,)), ((), ()))
    nt = (((1,), (1,)), ((), ()))

    def finish(chunk):
        rows = slice(chunk * c, (chunk + 1) * c)
        out = part_ref[rows, :] + cross_ref[rows, :]
        o_ref[rows, :] = (rg_ref[rows, :].astype(F32) * (_rms_scale(out) * nrm)).astype(BF16)

    sf = sb = jnp.zeros((RET_DK, RET_DV), F32)
    for i in range(nc):
        rows = slice(i * c, (i + 1) * c)
        q = q_ref[rows, :]
        k = k_ref[rows, :]
        v = v_ref[rows, :]
        scores = lax.dot_general(q, k, nt, preferred_element_type=F32) * decay
        part = jnp.dot(scores.astype(BF16), v, preferred_element_type=F32)
        part += jnp.dot((q.astype(F32) * qf_dec).astype(BF16), sf.astype(BF16), preferred_element_type=F32)
        part_ref[rows, :] = part
        sf = sf * cdf + lax.dot_general((k.astype(F32) * kf_dec).astype(BF16), v, tn, preferred_element_type=F32)

        rows = slice((nc - 1 - i) * c, (nc - i) * c)
        q = q_ref[rows, :].astype(F32)
        k = k_ref[rows, :].astype(F32)
        cross_ref[rows, :] = jnp.dot((q * qb_dec).astype(BF16), sb.astype(BF16), preferred_element_type=F32)
        sb = sb * cdb + lax.dot_general((k * kb_dec).astype(BF16), v_ref[rows, :], tn, preferred_element_type=F32)

        if 2 * i + 1 >= nc:
            for chunk in sorted({i, nc - 1 - i}):
                finish(chunk)


def _retention(lgf, lgb, rq, rk, rv, rg, nrm, batch):
    n = rq.shape[0]
    bh = lambda b, h, *_: (b, h)
    return pl.pallas_call(
        _retention_kernel,
        grid_spec=pltpu.PrefetchScalarGridSpec(
            num_scalar_prefetch=2,
            grid=(batch, RET_HEADS),
            in_specs=[pl.BlockSpec((SEQ, RET_DK), bh), pl.BlockSpec((SEQ, RET_DK), bh),
                      pl.BlockSpec((SEQ, RET_DV), bh), pl.BlockSpec((SEQ, RET_DV), bh),
                      pl.BlockSpec((1, RET_DV), lambda b, h, *_: (0, h))],
            out_specs=pl.BlockSpec((SEQ, RET_DV), bh),
            scratch_shapes=[pltpu.VMEM((SEQ, RET_DV), F32), pltpu.VMEM((SEQ, RET_DV), F32)],
        ),
        out_shape=jax.ShapeDtypeStruct((n, RV), BF16),
        compiler_params=_params("parallel", "parallel"),
        name="retention",
    )(lgf, lgb, rq, rk, rv, rg, nrm)


def _attn_block(q, kw, vw, bias):
    lane = lax.broadcasted_iota(jnp.int32, q.shape, 1)
    head0_qk = (lane < 32) | ((lane >= 64) & (lane < 96))
    outs, maxs, sums = [], [], []
    for hh in range(2):
        qm = jnp.where(head0_qk if hh == 0 else jnp.logical_not(head0_qk), q, jnp.zeros_like(q))
        s = lax.dot_general(qm, kw, (((1,), (1,)), ((), ())), preferred_element_type=F32) + bias
        m = jnp.max(s, axis=-1, keepdims=True)
        p = jnp.exp(s - m)
        outs.append(jnp.dot(p.astype(BF16), vw, preferred_element_type=F32))
        maxs.append(m)
        sums.append(jnp.sum(p, axis=-1, keepdims=True))
    head0_v = lane < ATT_DH
    return (jnp.where(head0_v, outs[0], outs[1]), jnp.where(head0_v, maxs[0], maxs[1]),
            jnp.where(head0_v, sums[0], sums[1]))


def _attention_kernel(q0, k0, v0, q1, k1, v1, q2, k2, v2, o_ref, o1_ref, m1_ref, l1_ref, o2_ref, m2_ref, l2_ref,
                      bias_ref):
    qb, kw = ATT_QB, ATT_KW
    ji = lax.broadcasted_iota(jnp.int32, (qb, kw), 1) - lax.broadcasted_iota(jnp.int32, (qb, kw), 0)
    for n in range(3):
        rel = ji - n * ATT_HALF_SPAN
        bias_ref[n] = jnp.where((rel >= -ATT_HALF_SPAN) & (rel <= ATT_HALF_SPAN), 0.0, MASK_NEG).astype(F32)

    def group(q_ref, k_ref, v_ref, dil, finish):
        sub_len = SEQ // dil
        nblk = sub_len // qb

        def body(jt, carry):
            for u in range(ATT_UNROLL):
                it = jt * ATT_UNROLL + u
                r = it // nblk
                l0 = (it % nblk) * qb
                kstart = pl.multiple_of(jnp.clip(l0 - ATT_HALF_SPAN, 0, sub_len - kw), ATT_HALF_SPAN)
                bias = bias_ref[(l0 - kstart) // ATT_HALF_SPAN]
                q = q_ref[0, 0, r, pl.ds(pl.multiple_of(l0, qb), qb), :]
                kwin = k_ref[0, 0, r, pl.ds(kstart, kw), :]
                vwin = v_ref[0, 0, r, pl.ds(kstart, kw), :]
                finish(l0, r, *_attn_block(q, kwin, vwin, bias))
            return carry

        lax.fori_loop(0, dil * nblk // ATT_UNROLL, body, 0)

    def store_strided(o_nat, m_nat, l_nat, dil):
        def finish(l0, r, o, m, l):
            rows = pl.ds(l0 * dil + r, ATT_QB, stride=dil)
            o_nat[rows, :] = o
            m_nat[rows, :] = m
            l_nat[rows, :] = l
        return finish

    group(q2, k2, v2, ATT_DILATIONS[2], store_strided(o2_ref, m2_ref, l2_ref, ATT_DILATIONS[2]))
    group(q1, k1, v1, ATT_DILATIONS[1], store_strided(o1_ref, m1_ref, l1_ref, ATT_DILATIONS[1]))

    def merge(l0, r, o, m, l):
        rows = pl.ds(pl.multiple_of(l0, ATT_QB), ATT_QB)
        ma, mb = m1_ref[rows, :], m2_ref[rows, :]
        mx = jnp.maximum(m, jnp.maximum(ma, mb))
        w0, w1, w2 = jnp.exp(m - mx), jnp.exp(ma - mx), jnp.exp(mb - mx)
        num = w0 * o + w1 * o1_ref[rows, :] + w2 * o2_ref[rows, :]
        den = w0 * l + w1 * l1_ref[rows, :] + w2 * l2_ref[rows, :]
        o_ref[rows, :] = (num / den).astype(BF16)

    group(q0, k0, v0, ATT_DILATIONS[0], merge)


def _attention(qkv, batch):
    in_specs = []
    for g, dil in enumerate(ATT_DILATIONS):
        for _ in range(3):
            in_specs.append(pl.BlockSpec((1, 1, dil, SEQ // dil, LANES), lambda b, p: (b, p, 0, 0, 0)))
    args = []
    for g in range(3):
        args += [qkv[0 * 3 + g], qkv[1 * 3 + g], qkv[2 * 3 + g]]
    return pl.pallas_call(
        _attention_kernel,
        grid=(batch, ATT_PAIRS),
        in_specs=in_specs,
        out_specs=pl.BlockSpec((SEQ, LANES), lambda b, p: (b, p)),
        out_shape=jax.ShapeDtypeStruct((batch * SEQ, A_OUT), BF16),
        scratch_shapes=[pltpu.VMEM((SEQ, LANES), F32) for _ in range(6)]
        + [pltpu.VMEM((3, ATT_QB, ATT_KW), F32)],
        compiler_params=_params("parallel", "parallel"),
        name="dilated_attention",
    )(*args)


def _mix_out_kernel(a_ref, att_ref, g_ref, x_ref, wr_ref, wa_ref, wo_ref, nf_ref, rhi_ref, rlo_ref,
                    x1_ref, h2_ref, aff_ref):
    bra = jnp.dot(a_ref[...], wr_ref[...], preferred_element_type=F32)
    brb = jnp.dot(att_ref[...], wa_ref[...], preferred_element_type=F32)
    g = g_ref[...].astype(F32)
    merged = g[:, :D_MODEL] * bra + g[:, D_MODEL:] * brb
    x1 = x_ref[...] + jnp.dot(merged.astype(BF16), wo_ref[...], preferred_element_type=F32)
    x1_ref[...] = x1
    h2 = _rms_scale(x1) * nf_ref[...]
    hi = h2.astype(BF16)
    lo = (h2 - hi.astype(F32)).astype(BF16)
    nt = (((1,), (1,)), ((), ()))
    logits = (lax.dot_general(rhi_ref[...], hi, nt, preferred_element_type=F32)
              + lax.dot_general(rhi_ref[...], lo, nt, preferred_element_type=F32)
              + lax.dot_general(rlo_ref[...], hi, nt, preferred_element_type=F32))
    e = jnp.exp(logits - jnp.max(logits, axis=0, keepdims=True))
    aff = e / jnp.sum(e, axis=0, keepdims=True)
    aff_ref[...] = aff
    h2_ref[:, :D_MODEL] = h2
    pad = jnp.zeros((LANES - N_EXPERTS, aff.shape[1]), F32)
    h2_ref[:, D_MODEL:] = jnp.transpose(jnp.concatenate([aff, pad], axis=0))


def _mix_out(a, att, g, x2d, wr, wa, wo, nf, rhi, rlo):
    n = x2d.shape[0]
    tm = PROJ_TM
    row = lambda i: (i, 0)
    fixed = lambda i: (0, 0)
    return pl.pallas_call(
        _mix_out_kernel,
        grid=(n // tm,),
        in_specs=[pl.BlockSpec((tm, RV), row), pl.BlockSpec((tm, A_OUT), row), pl.BlockSpec((tm, 2 * D_MODEL), row),
                  pl.BlockSpec((tm, D_MODEL), row), _resident(wr.shape, fixed), _resident(wa.shape, fixed),
                  _resident(wo.shape, fixed), pl.BlockSpec((1, D_MODEL), fixed),
                  pl.BlockSpec(rhi.shape, fixed), pl.BlockSpec(rlo.shape, fixed)],
        out_specs=[pl.BlockSpec((tm, D_MODEL), row), pl.BlockSpec((tm, TOK_ROW), row),
                   pl.BlockSpec((N_EXPERTS, tm), lambda i: (0, i))],
        out_shape=[jax.ShapeDtypeStruct((n, D_MODEL), F32), jax.ShapeDtypeStruct((n, TOK_ROW), F32),
                   jax.ShapeDtypeStruct((N_EXPERTS, n), F32)],
        compiler_params=_params("parallel"),
        name="mix_out",
    )(a, att, g, x2d, wr, wa, wo, nf, rhi, rlo)


def _topk_kernel(aff_ref, spos_ref, offs_ref, idx_ref, *, cap):
    nchunk = aff_ref.shape[1]
    sl = TOPK_SL
    li = lax.broadcasted_iota(jnp.int32, (LANES, LANES), 0)
    lj = lax.broadcasted_iota(jnp.int32, (LANES, LANES), 1)
    upper = jnp.where(li <= lj, 1.0, 0.0).astype(BF16)
    ci = lax.broadcasted_iota(jnp.int32, (nchunk, nchunk), 0)
    cj = lax.broadcasted_iota(jnp.int32, (nchunk, nchunk), 1)
    lower = jnp.where(cj < ci, 1.0, 0.0).astype(BF16)
    row_ids = lax.broadcasted_iota(jnp.int32, (sl, nchunk), 1).astype(F32)
    slot_iota = lax.broadcasted_iota(jnp.int32, (sl, 1), 0).astype(F32)

    def total(x):
        return jnp.sum(jnp.sum(x, axis=0, keepdims=True), axis=1, keepdims=True)

    def cumsum(mask):
        within = jnp.dot(mask.astype(BF16), upper, preferred_element_type=F32)
        tot = jnp.broadcast_to(within[:, LANES - 1:LANES], within.shape).astype(BF16)
        before = jnp.dot(lower, tot, preferred_element_type=F32)
        return within + before, before

    def expert_group(g, carry):
        experts = [g * TOPK_GROUP + u for u in range(TOPK_GROUP)]
        all_bits = [pltpu.bitcast(aff_ref[e], jnp.int32) for e in experts]

        def search(i, thrs):
            out = []
            for bits, thr in zip(all_bits, thrs):
                cand = thr | (1 << (30 - i))
                cnt = total(jnp.where(bits >= cand, 1, 0))
                out.append(jnp.where(cnt >= cap, cand, thr))
            return tuple(out)

        thrs = lax.fori_loop(0, 31, search, tuple(jnp.zeros((1, 1), jnp.int32) for _ in experts))
        inverters = [select(e, bits, thr) for e, bits, thr in zip(experts, all_bits, thrs)]

        def slot_tiles(jj, carry2):
            for u in range(TOPK_TILES):
                for invert in inverters:
                    invert(jj * TOPK_TILES + u)
            return carry2

        lax.fori_loop(0, cap // sl // TOPK_TILES, slot_tiles, 0)
        return carry

    def select(e, bits, thr):
        gt = bits > thr
        eq = bits == thr
        need = (cap - total(jnp.where(gt, 1, 0))).astype(F32)
        eq_f = jnp.where(eq, 1.0, 0.0)
        eq_cum, _ = cumsum(eq_f)
        sel = gt | (eq & (eq_cum - eq_f < need))
        cum, before = cumsum(jnp.where(sel, 1.0, 0.0))
        spos_ref[e] = jnp.where(sel, cum - 1.0, -1.0).astype(jnp.int32)
        offs_ref[e] = before.astype(jnp.int32)
        in_row = (cum - before).astype(BF16)
        ends = jnp.transpose(jnp.broadcast_to(cum[:, LANES - 1:LANES], cum.shape))[0:1, :]

        def slot_tile(jt):
            j = slot_iota + lax.convert_element_type(jt * sl, F32)
            done = ends <= j
            c_j = jnp.sum(jnp.where(done, 1.0, 0.0), axis=1, keepdims=True)
            before_j = jnp.max(jnp.where(done, ends, 0.0), axis=1, keepdims=True)
            pick = jnp.where(row_ids == c_j, 1.0, 0.0).astype(BF16)
            counts = jnp.dot(pick, in_row, preferred_element_type=F32)
            lane_j = jnp.sum(jnp.where(counts <= j - before_j, 1.0, 0.0), axis=1, keepdims=True)
            tok = c_j * float(LANES) + lane_j
            idx_ref[e, pl.ds(jt, 1), :] = jnp.transpose(jnp.broadcast_to(tok, (sl, LANES)))[0:1, :].astype(jnp.int32)

        return slot_tile

    lax.fori_loop(0, N_EXPERTS // TOPK_GROUP, expert_group, 0)


def _topk(aff3, cap):
    e, nchunk, _ = aff3.shape
    assert e % TOPK_GROUP == 0 and cap % (TOPK_SL * TOPK_TILES) == 0
    return pl.pallas_call(
        functools.partial(_topk_kernel, cap=cap),
        out_shape=[jax.ShapeDtypeStruct((e, nchunk, LANES), jnp.int32),
                   jax.ShapeDtypeStruct((e, nchunk, LANES), jnp.int32),
                   jax.ShapeDtypeStruct((e, cap // TOPK_SL, TOPK_SL), jnp.int32)],
        compiler_params=pltpu.CompilerParams(vmem_limit_bytes=VMEM_LIMIT),
        name="expert_choice_topk",
    )(aff3)


def _ffn_kernel(idx_ref, tok_hbm, wg_ref, wu_ref, wd_ref, out_ref, xs_ref, sem_ref):
    e = pl.program_id(0)
    s = pl.program_id(1)
    n_steps = pl.num_programs(0) * pl.num_programs(1)
    step = e * pl.num_programs(1) + s
    ts = FFN_TS

    def gather_rows(step_, buf, r0, r1):
        for r in range(r0, r1):
            pltpu.make_async_copy(tok_hbm.at[pl.ds(idx_ref[step_ * ts + r], 1), :],
                                  xs_ref.at[buf, pl.ds(r, 1), :], sem_ref.at[buf]).start()

    def wait_rows(buf):
        pltpu.make_async_copy(tok_hbm.at[pl.ds(0, ts), :], xs_ref.at[buf], sem_ref.at[buf]).wait()

    @pl.when(step == 0)
    def _():
        gather_rows(0, 0, 0, ts)

    def tile(buf):
        wait_rows(buf)
        gather_rows(jnp.minimum(step + 1, n_steps - 1), 1 - buf, 0, ts)
        xs = xs_ref[buf, :, :D_MODEL].astype(BF16)
        aff = xs_ref[buf, :, D_MODEL:]
        lane = lax.broadcasted_iota(jnp.int32, aff.shape, 1)
        route = jnp.sum(jnp.where(lane == e, aff, 0.0), axis=1, keepdims=True)
        gate = jnp.dot(xs, wg_ref[0], preferred_element_type=F32)
        up = jnp.dot(xs, wu_ref[0], preferred_element_type=F32)
        hid = (gate * _sigmoid(gate) * up).astype(BF16)
        out_ref[0] = (jnp.dot(hid, wd_ref[0], preferred_element_type=F32) * route).astype(BF16)

        @pl.when(step == n_steps - 1)
        def _():
            wait_rows(1 - buf)

    for parity in range(2):
        pl.when(step % 2 == parity)(functools.partial(tile, parity))


def _ffn(idx, tok, wg, wu, wd, cap):
    n_tiles = cap // FFN_TS
    wmap = lambda e, s, *_: (e, 0, 0)
    return pl.pallas_call(
        _ffn_kernel,
        grid_spec=pltpu.PrefetchScalarGridSpec(
            num_scalar_prefetch=1,
            grid=(N_EXPERTS, n_tiles),
            in_specs=[pl.BlockSpec(memory_space=pl.ANY),
                      pl.BlockSpec((1, D_MODEL, EXPERT_HIDDEN), wmap),
                      pl.BlockSpec((1, D_MODEL, EXPERT_HIDDEN), wmap),
                      pl.BlockSpec((1, EXPERT_HIDDEN, D_MODEL), wmap)],
            out_specs=pl.BlockSpec((1, FFN_TS, D_MODEL), lambda e, s, *_: (e, s, 0)),
            scratch_shapes=[pltpu.VMEM((2, FFN_TS, TOK_ROW), F32), pltpu.SemaphoreType.DMA((2,))],
        ),
        out_shape=jax.ShapeDtypeStruct((N_EXPERTS, cap, D_MODEL), BF16),
        compiler_params=_params("arbitrary", "arbitrary"),
        name="expert_ffn",
    )(idx, tok, wg, wu, wd)


def _combine_kernel(base_ref, spos_ref, x1_ref, nf_ref, eo_hbm, o_ref, stage_ref, sem_ref, tail_ref, tail_sem,
                    acc_ref, *, cap):
    i = pl.program_id(0)
    tm, win = CMB_TM, CMB_WIN
    row_ids = lax.broadcasted_iota(jnp.int32, (win, tm), 0)
    tn = (((0,), (0,)), ((), ()))

    def window_start(tile, e):
        return jnp.minimum((base_ref[tile * N_EXPERTS + e] // CMB_ALIGN) * CMB_ALIGN, cap - win)

    def window_copy(tile, e, buf):
        a0 = pl.multiple_of(window_start(tile, e), CMB_ALIGN)
        return pltpu.make_async_copy(eo_hbm.at[e, pl.ds(a0, win), :], stage_ref.at[buf, pl.ds(e * win, win), :],
                                     sem_ref.at[buf, e])

    @pl.when(i == 0)
    def _():
        for e in range(N_EXPERTS):
            window_copy(0, e, 0).start()

    @pl.when(i + 1 < pl.num_programs(0))
    def _():
        for e in range(N_EXPERTS):
            window_copy(i + 1, e, (i + 1) % 2).start()

    buf = i % 2
    pieces = []
    for e in range(N_EXPERTS):
        window_copy(i, e, buf).wait()
        pieces.append(jnp.where(spos_ref[e:e + 1, :] == window_start(i, e) + row_ids, 1.0, 0.0).astype(BF16))
    pick = jnp.concatenate(pieces, axis=0)
    acc_ref[...] = x1_ref[...] + lax.dot_general(pick, stage_ref[buf], tn, preferred_element_type=F32)

    for e in range(N_EXPERTS):
        end = base_ref[(i + 1) * N_EXPERTS + e]
        covered = window_start(i, e) + win

        @pl.when(end > covered)
        def _(e=e, end=end, covered=covered):
            def extra(w, carry):
                lo = covered + w * win
                a0 = pl.multiple_of(jnp.minimum(lo, cap - win), CMB_ALIGN)
                cp = pltpu.make_async_copy(eo_hbm.at[e, pl.ds(a0, win), :], tail_ref, tail_sem)
                cp.start()
                cp.wait()
                pos = spos_ref[e:e + 1, :]
                hit = (pos == a0 + row_ids) & (pos >= lo)
                acc_ref[...] += lax.dot_general(jnp.where(hit, 1.0, 0.0).astype(BF16), tail_ref[...], tn,
                                                preferred_element_type=F32)
                return carry

            lax.fori_loop(0, (end - covered + win - 1) // win, extra, 0)

    o_ref[...] = _rms_scale(acc_ref[...]) * nf_ref[...]


def _combine(base, spos, x1, nf, eo):
    n = x1.shape[0]
    cap = eo.shape[1]
    tm = CMB_TM
    row = lambda i, *_: (i, 0)
    return pl.pallas_call(
        functools.partial(_combine_kernel, cap=cap),
        grid_spec=pltpu.PrefetchScalarGridSpec(
            num_scalar_prefetch=1,
            grid=(n // tm,),
            in_specs=[pl.BlockSpec((N_EXPERTS, tm), lambda i, *_: (0, i)),
                      pl.BlockSpec((tm, D_MODEL), row), pl.BlockSpec((1, D_MODEL), lambda i, *_: (0, 0)),
                      pl.BlockSpec(memory_space=pl.ANY)],
            out_specs=pl.BlockSpec((tm, D_MODEL), row),
            scratch_shapes=[pltpu.VMEM((2, N_EXPERTS * CMB_WIN, D_MODEL), BF16),
                            pltpu.SemaphoreType.DMA((2, N_EXPERTS)),
                            pltpu.VMEM((CMB_WIN, D_MODEL), BF16), pltpu.SemaphoreType.DMA,
                            pltpu.VMEM((tm, D_MODEL), F32)],
        ),
        out_shape=jax.ShapeDtypeStruct((n, D_MODEL), F32),
        compiler_params=_params("arbitrary"),
        name="combine_final",
    )(base, spos, x1, nf, eo)


def _moe(x1, h2, aff_t, p):
    n = x1.shape[0]
    cap = EC_CAPACITY_FACTOR * n // N_EXPERTS
    spos, offs, idx = _topk(aff_t.reshape(N_EXPERTS, n // LANES, LANES), cap)
    eo = _ffn(idx.reshape(-1), h2, p["w_exp_gate"], p["w_exp_up"], p["w_exp_down"], cap)
    base = offs[:, ::CMB_TM // LANES, 0].T
    base = jnp.concatenate([base, jnp.full((1, N_EXPERTS), cap, jnp.int32)], axis=0).reshape(-1)
    return _combine(base, spos.reshape(N_EXPERTS, n), x1, p["norm_final"], eo)


def _pair_perm():
    base = np.concatenate([np.arange(0, 32), np.arange(64, 96), np.arange(32, 64), np.arange(96, 128)])
    return np.concatenate([base + LANES * b for b in range(AQ // LANES)])


def _rope_tables():
    pos = jnp.arange(SEQ, dtype=F32)
    ret_inv = 1.0 / (ROPE_THETA ** jnp.linspace(0.0, 1.0, RET_DK // 2, dtype=F32))
    ang = pos[:, None] * ret_inv[None, :]
    ret_cos = jnp.concatenate([jnp.cos(ang), jnp.cos(ang)], axis=-1)
    ret_sin = jnp.concatenate([-jnp.sin(ang), jnp.sin(ang)], axis=-1)
    att_inv = ROPE_THETA ** (-jnp.arange(0, ATT_DH, 2, dtype=F32) / ATT_DH)
    ang = pos[:, None] * att_inv[None, :]
    c, s = jnp.cos(ang), jnp.sin(ang)
    att_cos = jnp.concatenate([c, c, c, c], axis=-1)
    att_sin = jnp.concatenate([-s, -s, s, s], axis=-1)
    att_tabs = []
    for dil in ATT_DILATIONS:
        for tab in (att_cos, att_sin):
            att_tabs.append(tab.reshape(SEQ // dil, dil, LANES).transpose(1, 0, 2))
    return ret_cos, ret_sin, att_tabs


def _encoder(x, p):
    batch = x.shape[0]
    x2d = x.reshape(batch * SEQ, D_MODEL)
    rq, rk, rv, rg, gates = _proj_ret(x2d, p["norm_mix"], p["w_ret"], p["ret_cos"], p["ret_sin"])
    qkv = _proj_att(x2d, p["norm_mix"], p["w_att"], p["att_tabs"], batch)
    a = _retention(p["lgf"], p["lgb"], rq, rk, rv, rg, p["ret_norm"], batch)
    att = _attention(qkv, batch)
    x1, h2, aff_t = _mix_out(a, att, gates, x2d, p["w_ret_out"], p["w_att_out"], p["w_out"], p["norm_ffn"],
                             p["router_hi"], p["router_lo"])
    return _moe(x1, h2, aff_t, p).reshape(batch, SEQ, D_MODEL)


def kernel(x_prompt, x_sample, norm_mix, w_in, ret_log_decay_fwd, ret_log_decay_bwd, ret_norm, w_ret_out,
           w_att_out, w_out, norm_ffn, w_router, w_exp_gate, w_exp_up, w_exp_down, norm_final):
    w = w_in[0]
    c_rv = 2 * RQ
    c_rg = c_rv + RV
    c_aq = c_rg + RV
    c_ak = c_aq + AQ
    c_av = c_ak + AQ
    c_gate = c_av + AQ
    perm = _pair_perm()
    w_aq = w[:, c_aq:c_ak][:, perm] * (ATT_DH ** -0.5)
    w_ak = w[:, c_ak:c_av][:, perm]
    w_av = w[:, c_av:c_gate]
    ret_cos, ret_sin, att_tabs = _rope_tables()
    router_t = w_router[0].T
    router_hi = router_t.astype(BF16)
    p = {
        "norm_mix": norm_mix,
        "w_ret": jnp.concatenate([w[:, :c_aq], w[:, c_gate:]], axis=1).astype(BF16),
        "w_att": jnp.concatenate([w_aq, w_ak, w_av], axis=1).astype(BF16),
        "ret_cos": ret_cos, "ret_sin": ret_sin, "att_tabs": att_tabs,
        "lgf": ret_log_decay_fwd[0], "lgb": ret_log_decay_bwd[0],
        "ret_norm": ret_norm,
        "w_ret_out": w_ret_out[0].astype(BF16), "w_att_out": w_att_out[0].astype(BF16),
        "w_out": w_out[0].astype(BF16),
        "norm_ffn": norm_ffn,
        "router_hi": router_hi, "router_lo": (router_t - router_hi.astype(F32)).astype(BF16),
        "w_exp_gate": w_exp_gate[0].astype(BF16), "w_exp_up": w_exp_up[0].astype(BF16),
        "w_exp_down": w_exp_down[0].astype(BF16),
        "norm_final": norm_final.reshape(1, D_MODEL),
    }
    return _encoder(x_prompt, p), _encoder(x_sample, p)
```

```python
import functools

import numpy as np
import jax
import jax.numpy as jnp
from jax import lax
from jax.experimental import pallas as pl
from jax.experimental.pallas import tpu as pltpu

F32 = jnp.float32
BF16 = jnp.bfloat16

D_MODEL = 1024
SEQ = 4096
RET_HEADS = 4
RET_DK = 128
RET_DV = 256
RET_CHUNK = 256
ATT_DILATIONS = (1, 4, 16)
ATT_HALF_SPAN = 64
ATT_HEADS_PER_GROUP = 8
ATT_DH = 64
ATT_PAIRS = ATT_HEADS_PER_GROUP // 2
N_EXPERTS = 16
EC_CAPACITY_FACTOR = 2
EXPERT_HIDDEN = 2048
ROPE_THETA = 10000.0
EPS = 1e-6
RQ = RET_HEADS * RET_DK
RV = RET_HEADS * RET_DV
AQ = 3 * ATT_HEADS_PER_GROUP * ATT_DH
A_OUT = ATT_HEADS_PER_GROUP * ATT_DH
LANES = 128

PROJ_TM = 1024
ATT_QB = 128
ATT_KW = ATT_QB + 2 * ATT_HALF_SPAN
ATT_UNROLL = 32
FFN_TS = 256
TOK_ROW = D_MODEL + LANES
TOPK_SL = 512
TOPK_GROUP = 4
TOPK_TILES = 1
CMB_TM = 512
CMB_WIN = 112
CMB_ALIGN = 16
MASK_NEG = -1e30
VMEM_LIMIT = 56 * 1024 * 1024


def _params(*sem):
    return pltpu.CompilerParams(dimension_semantics=sem, vmem_limit_bytes=VMEM_LIMIT)


def _resident(shape, index_map):
    return pl.BlockSpec(shape, index_map, pipeline_mode=pl.Buffered(1))


def _sigmoid(x):
    return 1.0 / (1.0 + jnp.exp(-x))


def _rms_scale(x):
    return x * lax.rsqrt(jnp.mean(x * x, axis=-1, keepdims=True) + EPS)


def _rope(a, cos, sin):
    return a * cos + pltpu.roll(a, 64, 1) * sin


def _proj_ret_kernel(x_ref, nm_ref, w_ref, cos_ref, sin_ref, rq_ref, rk_ref, rv_ref, rg_ref, g_ref):
    h = (_rms_scale(x_ref[...]) * nm_ref[...]).astype(BF16)
    cos = cos_ref[...]
    sin = sin_ref[...]

    def proj(c0, width):
        return jnp.dot(h, w_ref[:, c0:c0 + width], preferred_element_type=F32)

    acc = proj(0, RQ)
    for hh in range(RET_HEADS):
        sl = slice(hh * LANES, (hh + 1) * LANES)
        rq_ref[:, sl] = _rope(acc[:, sl], cos, sin).astype(BF16)
    acc = proj(RQ, RQ)
    for hh in range(RET_HEADS):
        sl = slice(hh * LANES, (hh + 1) * LANES)
        rk_ref[:, sl] = (_rope(acc[:, sl], cos, sin) * (RET_DK ** -0.5)).astype(BF16)
    for c in range(RV // 512):
        rv_ref[:, c * 512:(c + 1) * 512] = proj(2 * RQ + c * 512, 512).astype(BF16)
    for c in range(RV // 512):
        acc = proj(2 * RQ + RV + c * 512, 512)
        rg_ref[:, c * 512:(c + 1) * 512] = (acc * _sigmoid(acc)).astype(BF16)
    for c in range(2 * D_MODEL // 512):
        acc = proj(2 * RQ + 2 * RV + c * 512, 512)
        g_ref[:, c * 512:(c + 1) * 512] = _sigmoid(acc).astype(BF16)


def _proj_ret(x2d, nm, w, cos, sin):
    n = x2d.shape[0]
    nt = SEQ // PROJ_TM
    tm = PROJ_TM
    row = lambda i: (i, 0)
    fixed = lambda i: (0, 0)
    tab = lambda i: (i % nt, 0)
    return pl.pallas_call(
        _proj_ret_kernel,
        grid=(n // tm,),
        in_specs=[pl.BlockSpec((tm, D_MODEL), row), pl.BlockSpec((1, D_MODEL), fixed),
                  _resident(w.shape, fixed), pl.BlockSpec((tm, LANES), tab), pl.BlockSpec((tm, LANES), tab)],
        out_specs=[pl.BlockSpec((tm, RQ), row), pl.BlockSpec((tm, RQ), row), pl.BlockSpec((tm, RV), row),
                   pl.BlockSpec((tm, RV), row), pl.BlockSpec((tm, 2 * D_MODEL), row)],
        out_shape=[jax.ShapeDtypeStruct((n, RQ), BF16), jax.ShapeDtypeStruct((n, RQ), BF16),
                   jax.ShapeDtypeStruct((n, RV), BF16), jax.ShapeDtypeStruct((n, RV), BF16),
                   jax.ShapeDtypeStruct((n, 2 * D_MODEL), BF16)],
        compiler_params=_params("parallel"),
        name="proj_ret",
    )(x2d, nm, w, cos, sin)


def _proj_att_kernel(x_ref, nm_ref, w_ref, c0_ref, s0_ref, c1_ref, s1_ref, c2_ref, s2_ref, *rest):
    out_refs = rest[:9]
    scr_ref = rest[9]
    h = (_rms_scale(x_ref[...]) * nm_ref[...]).astype(BF16)
    tabs = ((c0_ref, s0_ref), (c1_ref, s1_ref), (c2_ref, s2_ref))
    tm = x_ref.shape[0]
    for sec in range(3):
        for g, dil in enumerate(ATT_DILATIONS):
            c0 = (sec * 3 + g) * A_OUT
            acc = jnp.dot(h, w_ref[:, c0:c0 + A_OUT], preferred_element_type=F32)
            out_ref = out_refs[sec * 3 + g]
            if dil > 1:
                for p in range(ATT_PAIRS):
                    scr_ref[p] = acc[:, p * LANES:(p + 1) * LANES]
            for r in range(dil):
                for p in range(ATT_PAIRS):
                    if dil == 1:
                        a = acc[:, p * LANES:(p + 1) * LANES]
                    else:
                        a = scr_ref[p, pl.ds(r, tm // dil, stride=dil), :]
                    if sec < 2:
                        a = _rope(a, tabs[g][0][r], tabs[g][1][r])
                    out_ref[0, p, r] = a.astype(BF16)


def _proj_att(x2d, nm, w, tabs, batch):
    n = x2d.shape[0]
    tm = PROJ_TM
    nt = SEQ // tm
    row = lambda i: (i, 0)
    fixed = lambda i: (0, 0)
    in_specs = [pl.BlockSpec((tm, D_MODEL), row), pl.BlockSpec((1, D_MODEL), fixed), _resident(w.shape, fixed)]
    for dil in ATT_DILATIONS:
        for _ in range(2):
            in_specs.append(pl.BlockSpec((dil, tm // dil, LANES), lambda i: (0, i % nt, 0)))
    out_specs, out_shape = [], []
    for _ in range(3):
        for dil in ATT_DILATIONS:
            out_specs.append(pl.BlockSpec((1, ATT_PAIRS, dil, tm // dil, LANES),
                                          lambda i: (i // nt, 0, 0, i % nt, 0)))
            out_shape.append(jax.ShapeDtypeStruct((batch, ATT_PAIRS, dil, SEQ // dil, LANES), BF16))
    return pl.pallas_call(
        _proj_att_kernel,
        grid=(n // tm,),
        in_specs=in_specs,
        out_specs=out_specs,
        out_shape=out_shape,
        scratch_shapes=[pltpu.VMEM((ATT_PAIRS, tm, LANES), F32)],
        compiler_params=_params("parallel"),
        name="proj_att",
    )(x2d, nm, w, *tabs)


def _retention_kernel(lgf_ref, lgb_ref, q_ref, k_ref, v_ref, rg_ref, nrm_ref, o_ref, part_ref, cross_ref):
    hd = pl.program_id(1)
    lgf = lgf_ref[hd]
    lgb = lgb_ref[hd]
    c = RET_CHUNK
    nc = q_ref.shape[0] // c
    ti = lax.broadcasted_iota(jnp.int32, (c, c), 0)
    si = lax.broadcasted_iota(jnp.int32, (c, c), 1)
    diff = (ti - si).astype(F32)
    fwd = diff >= 0.0
    decay = jnp.where(fwd, jnp.exp(lgf * jnp.where(fwd, diff, 0.0)), jnp.exp(lgb * jnp.where(fwd, 0.0, -diff)))
    idx = lax.broadcasted_iota(jnp.int32, (c, 1), 0).astype(F32)
    qf_dec = jnp.exp(lgf * (idx + 1.0))
    kf_dec = jnp.exp(lgf * (c - 1.0 - idx))
    qb_dec = jnp.exp(lgb * (c - idx))
    kb_dec = jnp.exp(lgb * idx)
    one = jnp.ones((1, 1), F32)
    cdf = jnp.exp(one * (lgf * c))
    cdb = jnp.exp(one * (lgb * c))
    nrm = nrm_ref[...]
    tn = (((0,), (0,)), ((), ()))
    nt = (((1,), (1,)), ((), ()))

    def finish(chunk):
        rows = slice(chunk * c, (chunk + 1) * c)
        out = part_ref[rows, :] + cross_ref[rows, :]
        o_ref[rows, :] = (rg_ref[rows, :].astype(F32) * (_rms_scale(out) * nrm)).astype(BF16)

    sf = sb = jnp.zeros((RET_DK, RET_DV), F32)
    for i in range(nc):
        rows = slice(i * c, (i + 1) * c)
        q = q_ref[rows, :]
        k = k_ref[rows, :]
        v = v_ref[rows, :]
        scores = lax.dot_general(q, k, nt, preferred_element_type=F32) * decay
        part = jnp.dot(scores.astype(BF16), v, preferred_element_type=F32)
        part += jnp.dot((q.astype(F32) * qf_dec).astype(BF16), sf.astype(BF16), preferred_element_type=F32)
        part_ref[rows, :] = part
        sf = sf * cdf + lax.dot_general((k.astype(F32) * kf_dec).astype(BF16), v, tn, preferred_element_type=F32)

        rows = slice((nc - 1 - i) * c, (nc - i) * c)
        q = q_ref[rows, :].astype(F32)
        k = k_ref[rows, :].astype(F32)
        cross_ref[rows, :] = jnp.dot((q * qb_dec).astype(BF16), sb.astype(BF16), preferred_element_type=F32)
        sb = sb * cdb + lax.dot_general((k * kb_dec).astype(BF16), v_ref[rows, :], tn, preferred_element_type=F32)

        if 2 * i + 1 >= nc:
            for chunk in sorted({i, nc - 1 - i}):
                finish(chunk)


def _retention(lgf, lgb, rq, rk, rv, rg, nrm, batch):
    n = rq.shape[0]
    bh = lambda b, h, *_: (b, h)
    return pl.pallas_call(
        _retention_kernel,
        grid_spec=pltpu.PrefetchScalarGridSpec(
            num_scalar_prefetch=2,
            grid=(batch, RET_HEADS),
            in_specs=[pl.BlockSpec((SEQ, RET_DK), bh), pl.BlockSpec((SEQ, RET_DK), bh),
                      pl.BlockSpec((SEQ, RET_DV), bh), pl.BlockSpec((SEQ, RET_DV), bh),
                      pl.BlockSpec((1, RET_DV), lambda b, h, *_: (0, h))],
            out_specs=pl.BlockSpec((SEQ, RET_DV), bh),
            scratch_shapes=[pltpu.VMEM((SEQ, RET_DV), F32), pltpu.VMEM((SEQ, RET_DV), F32)],
        ),
        out_shape=jax.ShapeDtypeStruct((n, RV), BF16),
        compiler_params=_params("parallel", "parallel"),
        name="retention",
    )(lgf, lgb, rq, rk, rv, rg, nrm)


def _attn_block(q, kw, vw, bias):
    lane = lax.broadcasted_iota(jnp.int32, q.shape, 1)
    head0_qk = (lane < 32) | ((lane >= 64) & (lane < 96))
    outs, maxs, sums = [], [], []
    for hh in range(2):
        qm = jnp.where(head0_qk if hh == 0 else jnp.logical_not(head0_qk), q, jnp.zeros_like(q))
        s = lax.dot_general(qm, kw, (((1,), (1,)), ((), ())), preferred_element_type=F32) + bias
        m = jnp.max(s, axis=-1, keepdims=True)
        p = jnp.exp(s - m)
        outs.append(jnp.dot(p.astype(BF16), vw, preferred_element_type=F32))
        maxs.append(m)
        sums.append(jnp.sum(p, axis=-1, keepdims=True))
    head0_v = lane < ATT_DH
    return (jnp.where(head0_v, outs[0], outs[1]), jnp.where(head0_v, maxs[0], maxs[1]),
            jnp.where(head0_v, sums[0], sums[1]))


def _attention_kernel(*refs, n_cast):
    q0, k0, v0, q1, k1, v1, q2, k2, v2 = refs[:9]
    cast_in = refs[9:9 + n_cast]
    o_ref = refs[9 + n_cast]
    cast_out = refs[10 + n_cast:10 + 2 * n_cast]
    o1_ref, m1_ref, l1_ref, o2_ref, m2_ref, l2_ref, bias_ref = refs[10 + 2 * n_cast:]
    for src, dst in zip(cast_in, cast_out):
        dst[...] = src[...].astype(BF16)
    qb, kw = ATT_QB, ATT_KW
    ji = lax.broadcasted_iota(jnp.int32, (qb, kw), 1) - lax.broadcasted_iota(jnp.int32, (qb, kw), 0)
    for n in range(3):
        rel = ji - n * ATT_HALF_SPAN
        bias_ref[n] = jnp.where((rel >= -ATT_HALF_SPAN) & (rel <= ATT_HALF_SPAN), 0.0, MASK_NEG).astype(F32)

    def group(q_ref, k_ref, v_ref, dil, finish):
        sub_len = SEQ // dil
        nblk = sub_len // qb

        def body(jt, carry):
            for u in range(ATT_UNROLL):
                it = jt * ATT_UNROLL + u
                r = it // nblk
                l0 = (it % nblk) * qb
                kstart = pl.multiple_of(jnp.clip(l0 - ATT_HALF_SPAN, 0, sub_len - kw), ATT_HALF_SPAN)
                bias = bias_ref[(l0 - kstart) // ATT_HALF_SPAN]
                q = q_ref[0, 0, r, pl.ds(pl.multiple_of(l0, qb), qb), :]
                kwin = k_ref[0, 0, r, pl.ds(kstart, kw), :]
                vwin = v_ref[0, 0, r, pl.ds(kstart, kw), :]
                finish(l0, r, *_attn_block(q, kwin, vwin, bias))
            return carry

        lax.fori_loop(0, dil * nblk // ATT_UNROLL, body, 0)

    def store_strided(o_nat, m_nat, l_nat, dil):
        def finish(l0, r, o, m, l):
            rows = pl.ds(l0 * dil + r, ATT_QB, stride=dil)
            o_nat[rows, :] = o
            m_nat[rows, :] = m
            l_nat[rows, :] = l
        return finish

    group(q2, k2, v2, ATT_DILATIONS[2], store_strided(o2_ref, m2_ref, l2_ref, ATT_DILATIONS[2]))
    group(q1, k1, v1, ATT_DILATIONS[1], store_strided(o1_ref, m1_ref, l1_ref, ATT_DILATIONS[1]))

    def merge(l0, r, o, m, l):
        rows = pl.ds(pl.multiple_of(l0, ATT_QB), ATT_QB)
        ma, mb = m1_ref[rows, :], m2_ref[rows, :]
        mx = jnp.maximum(m, jnp.maximum(ma, mb))
        w0, w1, w2 = jnp.exp(m - mx), jnp.exp(ma - mx), jnp.exp(mb - mx)
        num = w0 * o + w1 * o1_ref[rows, :] + w2 * o2_ref[rows, :]
        den = w0 * l + w1 * l1_ref[rows, :] + w2 * l2_ref[rows, :]
        o_ref[rows, :] = (num / den).astype(BF16)

    group(q0, k0, v0, ATT_DILATIONS[0], merge)


def _attention(qkv, batch, cast=()):
    in_specs = []
    for g, dil in enumerate(ATT_DILATIONS):
        for _ in range(3):
            in_specs.append(pl.BlockSpec((1, 1, dil, SEQ // dil, LANES), lambda b, p: (b, p, 0, 0, 0)))
    args = []
    for g in range(3):
        args += [qkv[0 * 3 + g], qkv[1 * 3 + g], qkv[2 * 3 + g]]
    out_specs = [pl.BlockSpec((SEQ, LANES), lambda b, p: (b, p))]
    out_shape = [jax.ShapeDtypeStruct((batch * SEQ, A_OUT), BF16)]
    n_steps = batch * ATT_PAIRS
    for w in cast:
        experts, rows, cols = w.shape
        per = n_steps // experts
        assert per * experts == n_steps and rows % (8 * per) == 0
        spec = pl.BlockSpec((1, rows // per, cols),
                            lambda b, p, per=per: ((b * ATT_PAIRS + p) // per, (b * ATT_PAIRS + p) % per, 0))
        in_specs.append(spec)
        out_specs.append(spec)
        out_shape.append(jax.ShapeDtypeStruct(w.shape, BF16))
    outs = pl.pallas_call(
        functools.partial(_attention_kernel, n_cast=len(cast)),
        grid=(batch, ATT_PAIRS),
        in_specs=in_specs,
        out_specs=out_specs,
        out_shape=out_shape,
        scratch_shapes=[pltpu.VMEM((SEQ, LANES), F32) for _ in range(6)]
        + [pltpu.VMEM((3, ATT_QB, ATT_KW), F32)],
        compiler_params=_params("parallel", "parallel"),
        name="dilated_attention",
    )(*args, *cast)
    return outs[0], tuple(outs[1:])


def _mix_out_kernel(a_ref, att_ref, g_ref, x_ref, wr_ref, wa_ref, wo_ref, nf_ref, rhi_ref, rlo_ref,
                    x1_ref, h2_ref, aff_ref):
    bra = jnp.dot(a_ref[...], wr_ref[...], preferred_element_type=F32)
    brb = jnp.dot(att_ref[...], wa_ref[...], preferred_element_type=F32)
    g = g_ref[...].astype(F32)
    merged = g[:, :D_MODEL] * bra + g[:, D_MODEL:] * brb
    x1 = x_ref[...] + jnp.dot(merged.astype(BF16), wo_ref[...], preferred_element_type=F32)
    x1_ref[...] = x1
    h2 = _rms_scale(x1) * nf_ref[...]
    hi = h2.astype(BF16)
    lo = (h2 - hi.astype(F32)).astype(BF16)
    nt = (((1,), (1,)), ((), ()))
    logits = (lax.dot_general(rhi_ref[...], hi, nt, preferred_element_type=F32)
              + lax.dot_general(rhi_ref[...], lo, nt, preferred_element_type=F32)
              + lax.dot_general(rlo_ref[...], hi, nt, preferred_element_type=F32))
    e = jnp.exp(logits - jnp.max(logits, axis=0, keepdims=True))
    aff = e / jnp.sum(e, axis=0, keepdims=True)
    aff_ref[...] = aff
    h2_ref[:, :D_MODEL] = h2
    pad = jnp.zeros((LANES - N_EXPERTS, aff.shape[1]), F32)
    h2_ref[:, D_MODEL:] = jnp.transpose(jnp.concatenate([aff, pad], axis=0))


def _mix_out(a, att, g, x2d, wr, wa, wo, nf, rhi, rlo):
    n = x2d.shape[0]
    tm = PROJ_TM
    row = lambda i: (i, 0)
    fixed = lambda i: (0, 0)
    return pl.pallas_call(
        _mix_out_kernel,
        grid=(n // tm,),
        in_specs=[pl.BlockSpec((tm, RV), row), pl.BlockSpec((tm, A_OUT), row), pl.BlockSpec((tm, 2 * D_MODEL), row),
                  pl.BlockSpec((tm, D_MODEL), row), _resident(wr.shape, fixed), _resident(wa.shape, fixed),
                  _resident(wo.shape, fixed), pl.BlockSpec((1, D_MODEL), fixed),
                  pl.BlockSpec(rhi.shape, fixed), pl.BlockSpec(rlo.shape, fixed)],
        out_specs=[pl.BlockSpec((tm, D_MODEL), row), pl.BlockSpec((tm, TOK_ROW), row),
                   pl.BlockSpec((N_EXPERTS, tm), lambda i: (0, i))],
        out_shape=[jax.ShapeDtypeStruct((n, D_MODEL), F32), jax.ShapeDtypeStruct((n, TOK_ROW), F32),
                   jax.ShapeDtypeStruct((N_EXPERTS, n), F32)],
        compiler_params=_params("parallel"),
        name="mix_out",
    )(a, att, g, x2d, wr, wa, wo, nf, rhi, rlo)


def _topk_kernel(aff_ref, spos_ref, offs_ref, idx_ref, *, cap):
    nchunk = aff_ref.shape[1]
    sl = TOPK_SL
    li = lax.broadcasted_iota(jnp.int32, (LANES, LANES), 0)
    lj = lax.broadcasted_iota(jnp.int32, (LANES, LANES), 1)
    upper = jnp.where(li <= lj, 1.0, 0.0).astype(BF16)
    ci = lax.broadcasted_iota(jnp.int32, (nchunk, nchunk), 0)
    cj = lax.broadcasted_iota(jnp.int32, (nchunk, nchunk), 1)
    lower = jnp.where(cj < ci, 1.0, 0.0).astype(BF16)
    row_ids = lax.broadcasted_iota(jnp.int32, (sl, nchunk), 1).astype(F32)
    slot_iota = lax.broadcasted_iota(jnp.int32, (sl, 1), 0).astype(F32)

    def total(x):
        return jnp.sum(jnp.sum(x, axis=0, keepdims=True), axis=1, keepdims=True)

    def cumsum(mask):
        within = jnp.dot(mask.astype(BF16), upper, preferred_element_type=F32)
        tot = jnp.broadcast_to(within[:, LANES - 1:LANES], within.shape).astype(BF16)
        before = jnp.dot(lower, tot, preferred_element_type=F32)
        return within + before, before

    def expert_group(g, carry):
        experts = [g * TOPK_GROUP + u for u in range(TOPK_GROUP)]
        all_bits = [pltpu.bitcast(aff_ref[e], jnp.int32) for e in experts]

        def search(i, thrs):
            out = []
            for bits, thr in zip(all_bits, thrs):
                cand = thr | (1 << (30 - i))
                cnt = total(jnp.where(bits >= cand, 1, 0))
                out.append(jnp.where(cnt >= cap, cand, thr))
            return tuple(out)

        thrs = lax.fori_loop(0, 31, search, tuple(jnp.zeros((1, 1), jnp.int32) for _ in experts))
        inverters = [select(e, bits, thr) for e, bits, thr in zip(experts, all_bits, thrs)]

        def slot_tiles(jj, carry2):
            for u in range(TOPK_TILES):
                for invert in inverters:
                    invert(jj * TOPK_TILES + u)
            return carry2

        lax.fori_loop(0, cap // sl // TOPK_TILES, slot_tiles, 0)
        return carry

    def select(e, bits, thr):
        gt = bits > thr
        eq = bits == thr
        need = (cap - total(jnp.where(gt, 1, 0))).astype(F32)
        eq_f = jnp.where(eq, 1.0, 0.0)
        eq_cum, _ = cumsum(eq_f)
        sel = gt | (eq & (eq_cum - eq_f < need))
        cum, before = cumsum(jnp.where(sel, 1.0, 0.0))
        spos_ref[e] = jnp.where(sel, cum - 1.0, -1.0).astype(jnp.int32)
        offs_ref[e] = before.astype(jnp.int32)
        in_row = (cum - before).astype(BF16)
        ends = jnp.transpose(jnp.broadcast_to(cum[:, LANES - 1:LANES], cum.shape))[0:1, :]

        def slot_tile(jt):
            j = slot_iota + lax.convert_element_type(jt * sl, F32)
            done = ends <= j
            c_j = jnp.sum(jnp.where(done, 1.0, 0.0), axis=1, keepdims=True)
            before_j = jnp.max(jnp.where(done, ends, 0.0), axis=1, keepdims=True)
            pick = jnp.where(row_ids == c_j, 1.0, 0.0).astype(BF16)
            counts = jnp.dot(pick, in_row, preferred_element_type=F32)
            lane_j = jnp.sum(jnp.where(counts <= j - before_j, 1.0, 0.0), axis=1, keepdims=True)
            tok = c_j * float(LANES) + lane_j
            idx_ref[e, pl.ds(jt, 1), :] = jnp.transpose(jnp.broadcast_to(tok, (sl, LANES)))[0:1, :].astype(jnp.int32)

        return slot_tile

    lax.fori_loop(0, N_EXPERTS // TOPK_GROUP, expert_group, 0)


def _topk(aff3, cap):
    e, nchunk, _ = aff3.shape
    assert e % TOPK_GROUP == 0 and cap % (TOPK_SL * TOPK_TILES) == 0
    return pl.pallas_call(
        functools.partial(_topk_kernel, cap=cap),
        out_shape=[jax.ShapeDtypeStruct((e, nchunk, LANES), jnp.int32),
                   jax.ShapeDtypeStruct((e, nchunk, LANES), jnp.int32),
                   jax.ShapeDtypeStruct((e, cap // TOPK_SL, TOPK_SL), jnp.int32)],
        compiler_params=pltpu.CompilerParams(vmem_limit_bytes=VMEM_LIMIT),
        name="expert_choice_topk",
    )(aff3)


def _ffn_kernel(idx_ref, tok_hbm, wg_ref, wu_ref, wd_ref, out_ref, xs_ref, sem_ref):
    e = pl.program_id(0)
    s = pl.program_id(1)
    n_steps = pl.num_programs(0) * pl.num_programs(1)
    step = e * pl.num_programs(1) + s
    ts = FFN_TS

    def gather_rows(step_, buf, r0, r1):
        for r in range(r0, r1):
            pltpu.make_async_copy(tok_hbm.at[pl.ds(idx_ref[step_ * ts + r], 1), :],
                                  xs_ref.at[buf, pl.ds(r, 1), :], sem_ref.at[buf]).start()

    def wait_rows(buf):
        pltpu.make_async_copy(tok_hbm.at[pl.ds(0, ts), :], xs_ref.at[buf], sem_ref.at[buf]).wait()

    @pl.when(step == 0)
    def _():
        gather_rows(0, 0, 0, ts)

    def tile(buf):
        wait_rows(buf)
        gather_rows(jnp.minimum(step + 1, n_steps - 1), 1 - buf, 0, ts)
        xs = xs_ref[buf, :, :D_MODEL].astype(BF16)
        aff = xs_ref[buf, :, D_MODEL:]
        lane = lax.broadcasted_iota(jnp.int32, aff.shape, 1)
        route = jnp.sum(jnp.where(lane == e, aff, 0.0), axis=1, keepdims=True)
        gate = jnp.dot(xs, wg_ref[0], preferred_element_type=F32)
        up = jnp.dot(xs, wu_ref[0], preferred_element_type=F32)
        hid = (gate * _sigmoid(gate) * up).astype(BF16)
        out_ref[0] = (jnp.dot(hid, wd_ref[0], preferred_element_type=F32) * route).astype(BF16)

        @pl.when(step == n_steps - 1)
        def _():
            wait_rows(1 - buf)

    for parity in range(2):
        pl.when(step % 2 == parity)(functools.partial(tile, parity))


def _ffn(idx, tok, wg, wu, wd, cap):
    n_tiles = cap // FFN_TS
    wmap = lambda e, s, *_: (e, 0, 0)
    return pl.pallas_call(
        _ffn_kernel,
        grid_spec=pltpu.PrefetchScalarGridSpec(
            num_scalar_prefetch=1,
            grid=(N_EXPERTS, n_tiles),
            in_specs=[pl.BlockSpec(memory_space=pl.ANY),
                      pl.BlockSpec((1, D_MODEL, EXPERT_HIDDEN), wmap),
                      pl.BlockSpec((1, D_MODEL, EXPERT_HIDDEN), wmap),
                      pl.BlockSpec((1, EXPERT_HIDDEN, D_MODEL), wmap)],
            out_specs=pl.BlockSpec((1, FFN_TS, D_MODEL), lambda e, s, *_: (e, s, 0)),
            scratch_shapes=[pltpu.VMEM((2, FFN_TS, TOK_ROW), F32), pltpu.SemaphoreType.DMA((2,))],
        ),
        out_shape=jax.ShapeDtypeStruct((N_EXPERTS, cap, D_MODEL), BF16),
        compiler_params=_params("arbitrary", "arbitrary"),
        name="expert_ffn",
    )(idx, tok, wg, wu, wd)


def _combine_kernel(base_ref, spos_ref, x1_ref, nf_ref, eo_hbm, o_ref, stage_ref, sem_ref, tail_ref, tail_sem,
                    acc_ref, *, cap):
    i = pl.program_id(0)
    tm, win = CMB_TM, CMB_WIN
    row_ids = lax.broadcasted_iota(jnp.int32, (win, tm), 0)
    tn = (((0,), (0,)), ((), ()))

    def window_start(tile, e):
        return jnp.minimum((base_ref[tile * N_EXPERTS + e] // CMB_ALIGN) * CMB_ALIGN, cap - win)

    def window_copy(tile, e, buf):
        a0 = pl.multiple_of(window_start(tile, e), CMB_ALIGN)
        return pltpu.make_async_copy(eo_hbm.at[e, pl.ds(a0, win), :], stage_ref.at[buf, pl.ds(e * win, win), :],
                                     sem_ref.at[buf, e])

    @pl.when(i == 0)
    def _():
        for e in range(N_EXPERTS):
            window_copy(0, e, 0).start()

    @pl.when(i + 1 < pl.num_programs(0))
    def _():
        for e in range(N_EXPERTS):
            window_copy(i + 1, e, (i + 1) % 2).start()

    buf = i % 2
    pieces = []
    for e in range(N_EXPERTS):
        window_copy(i, e, buf).wait()
        pieces.append(jnp.where(spos_ref[e:e + 1, :] == window_start(i, e) + row_ids, 1.0, 0.0).astype(BF16))
    pick = jnp.concatenate(pieces, axis=0)
    acc_ref[...] = x1_ref[...] + lax.dot_general(pick, stage_ref[buf], tn, preferred_element_type=F32)

    for e in range(N_EXPERTS):
        end = base_ref[(i + 1) * N_EXPERTS + e]
        covered = window_start(i, e) + win

        @pl.when(end > covered)
        def _(e=e, end=end, covered=covered):
            def extra(w, carry):
                lo = covered + w * win
                a0 = pl.multiple_of(jnp.minimum(lo, cap - win), CMB_ALIGN)
                cp = pltpu.make_async_copy(eo_hbm.at[e, pl.ds(a0, win), :], tail_ref, tail_sem)
                cp.start()
                cp.wait()
                pos = spos_ref[e:e + 1, :]
                hit = (pos == a0 + row_ids) & (pos >= lo)
                acc_ref[...] += lax.dot_general(jnp.where(hit, 1.0, 0.0).astype(BF16), tail_ref[...], tn,
                                                preferred_element_type=F32)
                return carry

            lax.fori_loop(0, (end - covered + win - 1) // win, extra, 0)

    o_ref[...] = _rms_scale(acc_ref[...]) * nf_ref[...]


def _combine(base, spos, x1, nf, eo):
    n = x1.shape[0]
    cap = eo.shape[1]
    tm = CMB_TM
    row = lambda i, *_: (i, 0)
    return pl.pallas_call(
        functools.partial(_combine_kernel, cap=cap),
        grid_spec=pltpu.PrefetchScalarGridSpec(
            num_scalar_prefetch=1,
            grid=(n // tm,),
            in_specs=[pl.BlockSpec((N_EXPERTS, tm), lambda i, *_: (0, i)),
                      pl.BlockSpec((tm, D_MODEL), row), pl.BlockSpec((1, D_MODEL), lambda i, *_: (0, 0)),
                      pl.BlockSpec(memory_space=pl.ANY)],
            out_specs=pl.BlockSpec((tm, D_MODEL), row),
            scratch_shapes=[pltpu.VMEM((2, N_EXPERTS * CMB_WIN, D_MODEL), BF16),
                            pltpu.SemaphoreType.DMA((2, N_EXPERTS)),
                            pltpu.VMEM((CMB_WIN, D_MODEL), BF16), pltpu.SemaphoreType.DMA,
                            pltpu.VMEM((tm, D_MODEL), F32)],
        ),
        out_shape=jax.ShapeDtypeStruct((n, D_MODEL), F32),
        compiler_params=_params("arbitrary"),
        name="combine_final",
    )(base, spos, x1, nf, eo)


def _moe(x1, h2, aff_t, p, expert_w):
    n = x1.shape[0]
    cap = EC_CAPACITY_FACTOR * n // N_EXPERTS
    spos, offs, idx = _topk(aff_t.reshape(N_EXPERTS, n // LANES, LANES), cap)
    eo = _ffn(idx.reshape(-1), h2, *expert_w, cap)
    base = offs[:, ::CMB_TM // LANES, 0].T
    base = jnp.concatenate([base, jnp.full((1, N_EXPERTS), cap, jnp.int32)], axis=0).reshape(-1)
    return _combine(base, spos.reshape(N_EXPERTS, n), x1, p["norm_final"], eo)


def _pair_perm():
    base = np.concatenate([np.arange(0, 32), np.arange(64, 96), np.arange(32, 64), np.arange(96, 128)])
    return np.concatenate([base + LANES * b for b in range(AQ // LANES)])


def _rope_tables():
    pos = jnp.arange(SEQ, dtype=F32)
    ret_inv = 1.0 / (ROPE_THETA ** jnp.linspace(0.0, 1.0, RET_DK // 2, dtype=F32))
    ang = pos[:, None] * ret_inv[None, :]
    ret_cos = jnp.concatenate([jnp.cos(ang), jnp.cos(ang)], axis=-1)
    ret_sin = jnp.concatenate([-jnp.sin(ang), jnp.sin(ang)], axis=-1)
    att_inv = ROPE_THETA ** (-jnp.arange(0, ATT_DH, 2, dtype=F32) / ATT_DH)
    ang = pos[:, None] * att_inv[None, :]
    c, s = jnp.cos(ang), jnp.sin(ang)
    att_cos = jnp.concatenate([c, c, c, c], axis=-1)
    att_sin = jnp.concatenate([-s, -s, s, s], axis=-1)
    att_tabs = []
    for dil in ATT_DILATIONS:
        for tab in (att_cos, att_sin):
            att_tabs.append(tab.reshape(SEQ // dil, dil, LANES).transpose(1, 0, 2))
    return ret_cos, ret_sin, att_tabs


def _encoder(x, p, expert_w=None):
    batch = x.shape[0]
    x2d = x.reshape(batch * SEQ, D_MODEL)
    rq, rk, rv, rg, gates = _proj_ret(x2d, p["norm_mix"], p["w_ret"], p["ret_cos"], p["ret_sin"])
    qkv = _proj_att(x2d, p["norm_mix"], p["w_att"], p["att_tabs"], batch)
    a = _retention(p["lgf"], p["lgb"], rq, rk, rv, rg, p["ret_norm"], batch)
    att, cast = _attention(qkv, batch, () if expert_w else p["expert_w_f32"])
    expert_w = expert_w or cast
    x1, h2, aff_t = _mix_out(a, att, gates, x2d, p["w_ret_out"], p["w_att_out"], p["w_out"], p["norm_ffn"],
                             p["router_hi"], p["router_lo"])
    return _moe(x1, h2, aff_t, p, expert_w).reshape(batch, SEQ, D_MODEL), expert_w


def kernel(x_prompt, x_sample, norm_mix, w_in, ret_log_decay_fwd, ret_log_decay_bwd, ret_norm, w_ret_out,
           w_att_out, w_out, norm_ffn, w_router, w_exp_gate, w_exp_up, w_exp_down, norm_final):
    w = w_in[0]
    c_rv = 2 * RQ
    c_rg = c_rv + RV
    c_aq = c_rg + RV
    c_ak = c_aq + AQ
    c_av = c_ak + AQ
    c_gate = c_av + AQ
    perm = _pair_perm()
    w_aq = w[:, c_aq:c_ak][:, perm] * (ATT_DH ** -0.5)
    w_ak = w[:, c_ak:c_av][:, perm]
    w_av = w[:, c_av:c_gate]
    ret_cos, ret_sin, att_tabs = _rope_tables()
    router_t = w_router[0].T
    router_hi = router_t.astype(BF16)
    p = {
        "norm_mix": norm_mix,
        "w_ret": jnp.concatenate([w[:, :c_aq], w[:, c_gate:]], axis=1).astype(BF16),
        "w_att": jnp.concatenate([w_aq, w_ak, w_av], axis=1).astype(BF16),
        "ret_cos": ret_cos, "ret_sin": ret_sin, "att_tabs": att_tabs,
        "lgf": ret_log_decay_fwd[0], "lgb": ret_log_decay_bwd[0],
        "ret_norm": ret_norm,
        "w_ret_out": w_ret_out[0].astype(BF16), "w_att_out": w_att_out[0].astype(BF16),
        "w_out": w_out[0].astype(BF16),
        "norm_ffn": norm_ffn,
        "router_hi": router_hi, "router_lo": (router_t - router_hi.astype(F32)).astype(BF16),
        "expert_w_f32": (w_exp_gate[0], w_exp_up[0], w_exp_down[0]),
        "norm_final": norm_final.reshape(1, D_MODEL),
    }
    y_prompt, expert_w = _encoder(x_prompt, p)
    y_sample, _ = _encoder(x_sample, p, expert_w)
    return y_prompt, y_sample
```
